```python
import jax, jax.numpy as jnp
from jax import lax
import numpy as np

D_MODEL = 4096
BATCH = 1
SEQ = 8192
DEPTH = 4

CTX_LEN = 256
GRID_W = 64
N_MIXERS = 3
Q_BLOCK = 128
ROPE_THETA = 10000.0
EPS = 1e-6
N_MOD = 6

NA_HEADS = 32
NA_HEAD_DIM = D_MODEL // NA_HEADS
NA_WIN_H = 8
NA_WIN_W = 16

MLA_HEADS = 32
MLA_Q_RANK = 1024
MLA_KV_RANK = 512
MLA_NOPE = 128
MLA_ROPE = 64
MLA_V = 128

GQA_HEADS = 32
GQA_KV_HEADS = 8
GQA_HEAD_DIM = 128

PEER_HEADS = 8
PEER_N_KEYS = 96
PEER_N_EXPERTS = PEER_N_KEYS * PEER_N_KEYS
PEER_TOPK = 16
PEER_KEY_DIM = 256
PEER_HALF = PEER_KEY_DIM // 2
PEER_CHUNK = 64

DEEPNORM_ALPHA = (2 * DEPTH) ** 0.25
DEEPNORM_BETA = (8 * DEPTH) ** -0.25

kernel_name = "hybrid_natten_mla_gqa_peer_prefix_dit"


def layer_norm(x, g, b):
    xf = x.astype(jnp.float32)
    mu = jnp.mean(xf, -1, keepdims=True)
    var = jnp.mean(jnp.square(xf - mu), -1, keepdims=True)
    y = (xf - mu) * lax.rsqrt(var + 1e-5) * g.astype(jnp.float32) + b.astype(jnp.float32)
    return y.astype(x.dtype)


def rms_norm(x, g):
    xf = x.astype(jnp.float32)
    y = xf * lax.rsqrt(jnp.mean(jnp.square(xf), -1, keepdims=True) + EPS)
    return (y * g.astype(jnp.float32)).astype(x.dtype)


def rope_1d(x, pos):
    half = x.shape[-1] // 2
    inv = ROPE_THETA ** (-jnp.arange(half, dtype=jnp.float32) / half)
    ang = pos.astype(jnp.float32)[:, None] * inv[None, :]
    cos = jnp.cos(ang)[None, :, None, :]
    sin = jnp.sin(ang)[None, :, None, :]
    xf = x.astype(jnp.float32)
    x1, x2 = xf[..., :half], xf[..., half:]
    return jnp.concatenate([x1 * cos - x2 * sin, x1 * sin + x2 * cos], -1).astype(x.dtype)


def rope_2d(x, rows, cols):
    h = x.shape[-1] // 2
    return jnp.concatenate([rope_1d(x[..., :h], rows), rope_1d(x[..., h:], cols)], -1)


def to_blocks(a):
    b, s = a.shape[:2]
    return jnp.moveaxis(a.reshape(b, s // Q_BLOCK, Q_BLOCK, *a.shape[2:]), 1, 0)


def from_blocks(a):
    a = jnp.moveaxis(a, 0, 1)
    return a.reshape(a.shape[0], a.shape[1] * a.shape[2], *a.shape[3:])


def dense_attention(q, k, v, scale):
    b, _, hq, d = q.shape
    hkv = k.shape[2]
    g = hq // hkv
    dv = v.shape[-1]

    def block(qb):
        qb = qb.reshape(b, Q_BLOCK, hkv, g, d)
        s = jnp.einsum('bqkgd,bskd->bkgqs', qb, k).astype(jnp.float32) * scale
        p = jax.nn.softmax(s, axis=-1).astype(v.dtype)
        o = jnp.einsum('bkgqs,bskd->bqkgd', p, v)
        return o.reshape(b, Q_BLOCK, hq, dv)

    return from_blocks(lax.map(block, to_blocks(q)))


def neighbourhood_tables(s):
    rows = s // GRID_W
    kh = min(NA_WIN_H, rows)
    kw = NA_WIN_W
    t = jnp.arange(s, dtype=jnp.int32)
    r, col = t // GRID_W, t % GRID_W
    sr = jnp.clip(r - kh // 2, 0, rows - kh)
    sc = jnp.clip(col - kw // 2, 0, GRID_W - kw)
    kr = sr[:, None] + jnp.arange(kh, dtype=jnp.int32)
    kc = sc[:, None] + jnp.arange(kw, dtype=jnp.int32)
    idx = (kr[:, :, None] * GRID_W + kc[:, None, :]).reshape(s, kh * kw)
    dr = kr - r[:, None] + (NA_WIN_H - 1)
    dc = kc - col[:, None] + (NA_WIN_W - 1)
    bidx = (dr[:, :, None] * (2 * NA_WIN_W - 1) + dc[:, None, :]).reshape(s, kh * kw)
    return idx, bidx


def natten_mixer(hc, hl, w_qkv, rpb, w_o, nbr_idx, bias_idx, need_ctx):
    b = hl.shape[0]

    def proj(h):
        qkv = (h @ w_qkv).reshape(b, h.shape[1], 3, NA_HEADS, NA_HEAD_DIM)
        return qkv[:, :, 0], qkv[:, :, 1], qkv[:, :, 2]

    qc, kc, vc = proj(hc)
    ql, kl, vl = proj(hl)
    scale = NA_HEAD_DIM ** -0.5
    rpb_flat = rpb.reshape(NA_HEADS, -1).astype(jnp.float32)
    n_loc = nbr_idx.shape[1]

    def block(args):
        qb, ib, bb = args
        kg = kl[:, ib]
        vg = vl[:, ib]
        s_loc = jnp.einsum('bqhd,bqkhd->bhqk', qb, kg).astype(jnp.float32) * scale + rpb_flat[:, bb][None]
        s_ctx = jnp.einsum('bqhd,bchd->bhqc', qb, kc).astype(jnp.float32) * scale
        p = jax.nn.softmax(jnp.concatenate([s_loc, s_ctx], -1), axis=-1).astype(vl.dtype)
        return (jnp.einsum('bhqk,bqkhd->bqhd', p[..., :n_loc], vg)
                + jnp.einsum('bhqc,bchd->bqhd', p[..., n_loc:], vc))

    nb = ql.shape[1] // Q_BLOCK
    ol = from_blocks(lax.map(block, (to_blocks(ql),
                                     nbr_idx.reshape(nb, Q_BLOCK, n_loc),
                                     bias_idx.reshape(nb, Q_BLOCK, n_loc))))
    yl = ol.reshape(b, -1, NA_HEADS * NA_HEAD_DIM) @ w_o
    yc = None
    if need_ctx:
        yc = dense_attention(qc, kc, vc, scale).reshape(b, -1, NA_HEADS * NA_HEAD_DIM) @ w_o
    return yc, yl


def mla_mixer(hc, hl, w_dq, q_norm, w_uq, w_dkv, kv_norm, w_ukv, w_o, rows, cols, need_ctx):
    b = hl.shape[0]

    def proj(h, rotate):
        t = h.shape[1]
        q = (rms_norm(h @ w_dq, q_norm) @ w_uq).reshape(b, t, MLA_HEADS, MLA_NOPE + MLA_ROPE)
        q_nope, q_pe = q[..., :MLA_NOPE], q[..., MLA_NOPE:]
        ckv = h @ w_dkv
        c_kv, k_pe = ckv[..., :MLA_KV_RANK], ckv[..., MLA_KV_RANK:][:, :, None, :]
        kv = (rms_norm(c_kv, kv_norm) @ w_ukv).reshape(b, t, MLA_HEADS, MLA_NOPE + MLA_V)
        k_nope, v = kv[..., :MLA_NOPE], kv[..., MLA_NOPE:]
        if rotate:
            q_pe = rope_2d(q_pe, rows, cols)
            k_pe = rope_2d(k_pe, rows, cols)
        q = jnp.concatenate([q_nope, q_pe], -1)
        k = jnp.concatenate([k_nope, jnp.broadcast_to(k_pe, (b, t, MLA_HEADS, MLA_ROPE))], -1)
        return q, k, v

    qc, kc, vc = proj(hc, False)
    ql, kl, vl = proj(hl, True)
    scale = (MLA_NOPE + MLA_ROPE) ** -0.5
    ol = dense_attention(ql, jnp.concatenate([kl, kc], 1), jnp.concatenate([vl, vc], 1), scale)
    yl = ol.reshape(b, -1, MLA_HEADS * MLA_V) @ w_o
    yc = None
    if need_ctx:
        yc = dense_attention(qc, kc, vc, scale).reshape(b, -1, MLA_HEADS * MLA_V) @ w_o
    return yc, yl


def gqa_mixer(hc, hl, w_q, w_k, w_v, q_norm, k_norm, w_o, rows, cols, need_ctx):
    b = hl.shape[0]

    def proj(h, rotate):
        t = h.shape[1]
        q = rms_norm((h @ w_q).reshape(b, t, GQA_HEADS, GQA_HEAD_DIM), q_norm)
        k = rms_norm((h @ w_k).reshape(b, t, GQA_KV_HEADS, GQA_HEAD_DIM), k_norm)
        v = (h @ w_v).reshape(b, t, GQA_KV_HEADS, GQA_HEAD_DIM)
        if rotate:
            q = rope_2d(q, rows, cols)
            k = rope_2d(k, rows, cols)
        return q, k, v

    qc, kc, vc = proj(hc, False)
    ql, kl, vl = proj(hl, True)
    scale = GQA_HEAD_DIM ** -0.5
    ol = dense_attention(ql, jnp.concatenate([kl, kc], 1), jnp.concatenate([vl, vc], 1), scale)
    yl = ol.reshape(b, -1, GQA_HEADS * GQA_HEAD_DIM) @ w_o
    yc = None
    if need_ctx:
        yc = dense_attention(qc, kc, vc, scale).reshape(b, -1, GQA_HEADS * GQA_HEAD_DIM) @ w_o
    return yc, yl


def peer_ffn(h, w_query, sub_keys, u, v):
    b, t, d = h.shape
    n = b * t
    hf = h.reshape(n, d)
    q = (hf @ w_query).reshape(n, PEER_HEADS, 2, PEER_HALF)
    s = jnp.einsum('nphd,phkd->nphk', q, sub_keys).astype(jnp.float32)
    s1, i1 = lax.top_k(s[:, :, 0], PEER_TOPK)
    s2, i2 = lax.top_k(s[:, :, 1], PEER_TOPK)
    cand = (s1[..., :, None] + s2[..., None, :]).reshape(n, PEER_HEADS, PEER_TOPK * PEER_TOPK)
    best, bi = lax.top_k(cand, PEER_TOPK)
    expert = (jnp.take_along_axis(i1, bi // PEER_TOPK, -1) * PEER_N_KEYS
              + jnp.take_along_axis(i2, bi % PEER_TOPK, -1))
    gate = jax.nn.softmax(best, axis=-1)

    def chunk(args):
        xc, ec, gc = args
        a = jnp.einsum('cd,cpkd->cpk', xc, u[ec]).astype(jnp.float32)
        w = (jax.nn.gelu(a, approximate=False) * gc).astype(xc.dtype)
        return jnp.einsum('cpk,cpkd->cd', w, v[ec])

    nc = n // PEER_CHUNK
    y = lax.map(chunk, (hf.reshape(nc, PEER_CHUNK, d),
                        expert.reshape(nc, PEER_CHUNK, PEER_HEADS, PEER_TOPK),
                        gate.reshape(nc, PEER_CHUNK, PEER_HEADS, PEER_TOPK)))
    return y.reshape(b, t, d)


def setup_inputs(seed: int = 0) -> dict:
    key = jax.random.key(seed)
    ks = iter(jax.random.split(key, 40))

    def nrm(shape, scale):
        return jax.random.normal(next(ks), shape, jnp.float32) * scale

    n_a = len(range(0, DEPTH, N_MIXERS))
    n_b = len(range(1, DEPTH, N_MIXERS))
    n_c = len(range(2, DEPTH, N_MIXERS))
    D = D_MODEL
    beta = DEEPNORM_BETA
    return {
        "x": nrm((BATCH, SEQ, D), 1.0),
        "c": nrm((BATCH, D), 1.0),
        "ctx": nrm((BATCH, CTX_LEN, D), 1.0),
        "c_ctx": nrm((D,), 1.0),
        "ada_w": nrm((DEPTH, D, N_MOD * D), 0.5 * D ** -0.5),
        "ada_b": nrm((DEPTH, N_MOD * D), 0.01),
        "ln_g": 1.0 + nrm((DEPTH, 2, D), 0.02),
        "ln_b": nrm((DEPTH, 2, D), 0.02),
        "na_w_qkv": nrm((n_a, D, 3 * NA_HEADS * NA_HEAD_DIM), D ** -0.5),
        "na_rpb": nrm((n_a, NA_HEADS, 2 * NA_WIN_H - 1, 2 * NA_WIN_W - 1), 0.1),
        "na_w_o": nrm((n_a, NA_HEADS * NA_HEAD_DIM, D), beta * (NA_HEADS * NA_HEAD_DIM) ** -0.5),
        "mla_w_dq": nrm((n_b, D, MLA_Q_RANK), D ** -0.5),
        "mla_q_norm": 1.0 + nrm((n_b, MLA_Q_RANK), 0.02),
        "mla_w_uq": nrm((n_b, MLA_Q_RANK, MLA_HEADS * (MLA_NOPE + MLA_ROPE)), MLA_Q_RANK ** -0.5),
        "mla_w_dkv": nrm((n_b, D, MLA_KV_RANK + MLA_ROPE), D ** -0.5),
        "mla_kv_norm": 1.0 + nrm((n_b, MLA_KV_RANK), 0.02),
        "mla_w_ukv": nrm((n_b, MLA_KV_RANK, MLA_HEADS * (MLA_NOPE + MLA_V)), MLA_KV_RANK ** -0.5),
        "mla_w_o": nrm((n_b, MLA_HEADS * MLA_V, D), beta * (MLA_HEADS * MLA_V) ** -0.5),
        "gqa_w_q": nrm((n_c, D, GQA_HEADS * GQA_HEAD_DIM), D ** -0.5),
        "gqa_w_k": nrm((n_c, D, GQA_KV_HEADS * GQA_HEAD_DIM), D ** -0.5),
        "gqa_w_v": nrm((n_c, D, GQA_KV_HEADS * GQA_HEAD_DIM), D ** -0.5),
        "gqa_q_norm": 1.0 + nrm((n_c, GQA_HEAD_DIM), 0.02),
        "gqa_k_norm": 1.0 + nrm((n_c, GQA_HEAD_DIM), 0.02),
        "gqa_w_o": nrm((n_c, GQA_HEADS * GQA_HEAD_DIM, D), beta * (GQA_HEADS * GQA_HEAD_DIM) ** -0.5),
        "peer_w_query": nrm((DEPTH, D, PEER_HEADS * PEER_KEY_DIM), D ** -0.5),
        "peer_sub_keys": nrm((DEPTH, PEER_HEADS, 2, PEER_N_KEYS, PEER_HALF), PEER_HALF ** -0.5),
        "peer_u": nrm((DEPTH, PEER_N_EXPERTS, D), D ** -0.5),
        "peer_v": nrm((DEPTH, PEER_N_EXPERTS, D), beta),
    }


def reference(x, c, ctx, c_ctx, ada_w, ada_b, ln_g, ln_b,
              na_w_qkv, na_rpb, na_w_o,
              mla_w_dq, mla_q_norm, mla_w_uq, mla_w_dkv, mla_kv_norm, mla_w_ukv, mla_w_o,
              gqa_w_q, gqa_w_k, gqa_w_v, gqa_q_norm, gqa_k_norm, gqa_w_o,
              peer_w_query, peer_sub_keys, peer_u, peer_v):
    s = x.shape[1]
    t = jnp.arange(s, dtype=jnp.int32)
    rows, cols = t // GRID_W, t % GRID_W
    nbr_idx, bias_idx = neighbourhood_tables(s)
    n_ctx = ctx.shape[1]

    cond_l = jax.nn.silu(c)
    cond_c = jax.nn.silu(c_ctx)[None, :]
    xl, xc = x, ctx
    for i in range(DEPTH):
        kind, j = i % N_MIXERS, i // N_MIXERS
        need_ctx = i < DEPTH - 1
        sh_l, sc_l, g_l, fsh_l, fsc_l, fg_l = jnp.split((cond_l @ ada_w[i] + ada_b[i])[:, None, :], N_MOD, -1)
        sh_c, sc_c, g_c, fsh_c, fsc_c, fg_c = jnp.split((cond_c @ ada_w[i] + ada_b[i])[:, None, :], N_MOD, -1)

        hl = xl * (1 + sc_l) + sh_l
        hc = xc * (1 + sc_c) + sh_c
        if kind == 0:
            yc, yl = natten_mixer(hc, hl, na_w_qkv[j], na_rpb[j], na_w_o[j], nbr_idx, bias_idx, need_ctx)
        elif kind == 1:
            yc, yl = mla_mixer(hc, hl, mla_w_dq[j], mla_q_norm[j], mla_w_uq[j], mla_w_dkv[j],
                               mla_kv_norm[j], mla_w_ukv[j], mla_w_o[j], rows, cols, need_ctx)
        else:
            yc, yl = gqa_mixer(hc, hl, gqa_w_q[j], gqa_w_k[j], gqa_w_v[j], gqa_q_norm[j],
                               gqa_k_norm[j], gqa_w_o[j], rows, cols, need_ctx)
        xl = layer_norm(DEEPNORM_ALPHA * xl + g_l * yl, ln_g[i, 0], ln_b[i, 0])

        if need_ctx:
            xc = layer_norm(DEEPNORM_ALPHA * xc + g_c * yc, ln_g[i, 0], ln_b[i, 0])
            h = jnp.concatenate([xc * (1 + fsc_c) + fsh_c, xl * (1 + fsc_l) + fsh_l], 1)
            f = peer_ffn(h, peer_w_query[i], peer_sub_keys[i], peer_u[i], peer_v[i])
            fc, fl = f[:, :n_ctx], f[:, n_ctx:]
            xc = layer_norm(DEEPNORM_ALPHA * xc + fg_c * fc, ln_g[i, 1], ln_b[i, 1])
        else:
            fl = peer_ffn(xl * (1 + fsc_l) + fsh_l, peer_w_query[i], peer_sub_keys[i], peer_u[i], peer_v[i])
        xl = layer_norm(DEEPNORM_ALPHA * xl + fg_l * fl, ln_g[i, 1], ln_b[i, 1])
    return xl
```

```python
import functools
import math

import numpy as np
import jax
import jax.numpy as jnp
from jax import lax
from jax.experimental import pallas as pl
from jax.experimental.pallas import tpu as pltpu

F32 = jnp.float32
BF16 = jnp.bfloat16

GRID_W = 64
ROPE_THETA = 10000.0
RMS_EPS = 1e-6
LN_EPS = 1e-5
N_MOD = 6
NA_HEADS = 32
NA_WIN_H = 8
NA_WIN_W = 16
MLA_HEADS = 32
MLA_NOPE = 128
MLA_ROPE = 64
MLA_V = 128
MLA_KV_RANK = 512
GQA_HEADS = 32
GQA_KV_HEADS = 8
GQA_HEAD_DIM = 128
PEER_HEADS = 8
PEER_N_KEYS = 96
PEER_TOPK = 16
PEER_HALF = 128
N_MIXERS = 3

LANES = 128
MXU_DIM = 256
VMEM_LIMIT_CAP = 58 * 1024 * 1024

NEG_BIG = -1e30
LOG2E = 1.4426950408889634

NT_DIMS = (((1,), (1,)), ((), ()))
TN_DIMS = (((0,), (0,)), ((), ()))


def _largest_divisor(n, candidates):
    return next(c for c in candidates if n % c == 0)


def _cparams(sem, vmem_bytes):
    limit = int(min(VMEM_LIMIT_CAP, max(32 * 1024 * 1024, vmem_bytes + (8 << 20))))
    return pltpu.CompilerParams(dimension_semantics=sem, vmem_limit_bytes=limit)


def _adaln_kernel(c_ref, w_ref, b_ref, o_ref):
    c = c_ref[...]
    cond = c * jax.nn.sigmoid(c)
    o_ref[0] = jnp.dot(cond, w_ref[0], preferred_element_type=F32,
                       precision=lax.Precision.HIGHEST) + b_ref[0]


def _adaln(cmat, ada_w, ada_b):
    depth, d, n = ada_w.shape
    tn = 512
    return pl.pallas_call(
        _adaln_kernel,
        grid=(depth, n // tn),
        in_specs=[
            pl.BlockSpec((8, d), lambda i, j: (0, 0)),
            pl.BlockSpec((1, d, tn), lambda i, j: (i, 0, j)),
            pl.BlockSpec((1, 1, tn), lambda i, j: (i, 0, j)),
        ],
        out_specs=pl.BlockSpec((1, 8, tn), lambda i, j: (i, 0, j)),
        out_shape=jax.ShapeDtypeStruct((depth, 8, n), F32),
        compiler_params=_cparams(("parallel", "parallel"), 2 * d * tn * 4),
        name="adaln",
    )(cmat, ada_w, ada_b.reshape(depth, 1, n))


def _modulate_kernel(x_ref, sc_ref, sh_ref, h_ref):
    h_ref[...] = (x_ref[...] * (1.0 + sc_ref[0]) + sh_ref[0]).astype(h_ref.dtype)


def _modulate(x, sc, sh, *, n_ctx_blocks, tm):
    t, d = x.shape
    kind = lambda i: (jnp.where(i < n_ctx_blocks, 0, 1), 0, 0)
    return pl.pallas_call(
        _modulate_kernel,
        grid=(t // tm,),
        in_specs=[pl.BlockSpec((tm, d), lambda i: (i, 0)),
                  pl.BlockSpec((1, 1, d), kind), pl.BlockSpec((1, 1, d), kind)],
        out_specs=pl.BlockSpec((tm, d), lambda i: (i, 0)),
        out_shape=jax.ShapeDtypeStruct((t, d), BF16),
        compiler_params=_cparams(("parallel",), 2 * tm * d * 6),
        name="modulate",
    )(x, sc, sh)


def _ln_mod_kernel(*refs, alpha, with_h):
    if with_h:
        x_ref, y_ref, g_ref, lng_ref, lnb_ref, sc_ref, sh_ref, xo_ref, ho_ref = refs
    else:
        x_ref, y_ref, g_ref, lng_ref, lnb_ref, xo_ref = refs
    v = alpha * x_ref[...] + g_ref[0] * y_ref[...].astype(F32)
    mu = jnp.mean(v, axis=-1, keepdims=True)
    vc = v - mu
    var = jnp.mean(vc * vc, axis=-1, keepdims=True)
    xn = vc * lax.rsqrt(var + LN_EPS) * lng_ref[...] + lnb_ref[...]
    xo_ref[...] = xn
    if with_h:
        ho_ref[...] = (xn * (1.0 + sc_ref[0]) + sh_ref[0]).astype(ho_ref.dtype)


def _ln_mod(x, y, gate, lng, lnb, sc, sh, *, alpha, n_ctx_blocks, tm):
    t, d = x.shape
    with_h = sc is not None
    kind = lambda i: (jnp.where(i < n_ctx_blocks, 0, 1), 0, 0)
    row = pl.BlockSpec((tm, d), lambda i: (i, 0))
    vec = pl.BlockSpec((1, d), lambda i: (0, 0))
    in_specs = [row, row, pl.BlockSpec((1, 1, d), kind), vec, vec]
    args = [x, y, gate, lng.reshape(1, d), lnb.reshape(1, d)]
    out_shape = [jax.ShapeDtypeStruct((t, d), F32)]
    out_specs = [row]
    if with_h:
        in_specs += [pl.BlockSpec((1, 1, d), kind), pl.BlockSpec((1, 1, d), kind)]
        args += [sc, sh]
        out_shape.append(jax.ShapeDtypeStruct((t, d), BF16))
        out_specs.append(row)
    res = pl.pallas_call(
        functools.partial(_ln_mod_kernel, alpha=alpha, with_h=with_h),
        grid=(t // tm,),
        in_specs=in_specs, out_specs=out_specs, out_shape=out_shape,
        compiler_params=_cparams(("parallel",), 2 * tm * d * (4 + y.dtype.itemsize + 4 + 2)),
        name="ln_mod",
    )(*args)
    return (res[0], res[1]) if with_h else (res[0], None)


def _rope_rows_kernel(*refs, modes, half, scale, has_gain):
    if has_gain:
        x_ref, cos_ref, sin_ref, g_ref, o_ref = refs
    else:
        x_ref, cos_ref, sin_ref, o_ref = refs
    tm = x_ref.shape[0]
    cos = cos_ref[...]
    sin = sin_ref[...]
    lane = lax.broadcasted_iota(jnp.int32, (tm, LANES), 1)
    first = (lane % (2 * half)) < half
    for j in range(x_ref.shape[1] // LANES):
        mode = modes[j % len(modes)]
        x = x_ref[:, j * LANES:(j + 1) * LANES].astype(F32)
        if mode == "normrope":
            x = x * lax.rsqrt(jnp.mean(x * x, axis=-1, keepdims=True) + RMS_EPS) * g_ref[...]
        if mode in ("rope", "normrope"):
            partner = jnp.where(first, pltpu.roll(x, LANES - half, 1), pltpu.roll(x, half, 1))
            x = x * cos + partner * sin
        o_ref[:, j * LANES:(j + 1) * LANES] = (x * scale).astype(o_ref.dtype)


def _rope_rows(x, cos, sin, gain, *, modes, half, scale, tm):
    t, w = x.shape
    has_gain = gain is not None
    row = lambda width: pl.BlockSpec((tm, width), lambda i: (i, 0))
    in_specs = [row(w), row(LANES), row(LANES)]
    args = [x, cos, sin]
    if has_gain:
        in_specs.append(pl.BlockSpec((1, LANES), lambda i: (0, 0)))
        args.append(gain.reshape(1, LANES).astype(F32))
    return pl.pallas_call(
        functools.partial(_rope_rows_kernel, modes=modes, half=half, scale=scale, has_gain=has_gain),
        grid=(t // tm,),
        in_specs=in_specs, out_specs=row(w),
        out_shape=jax.ShapeDtypeStruct((t, w), BF16),
        compiler_params=_cparams(("parallel",), 2 * tm * w * (x.dtype.itemsize + 2)),
        name="rope_rows",
    )(*args)


def _mm_kernel(*refs, rms):
    if rms:
        a_ref, g_ref, b_ref, o_ref = refs
        x = a_ref[...].astype(F32)
        x = x * lax.rsqrt(jnp.mean(x * x, axis=-1, keepdims=True) + RMS_EPS) * g_ref[...]
        a = x.astype(BF16)
    else:
        a_ref, b_ref, o_ref = refs
        a = a_ref[...]
    o_ref[...] = jnp.dot(a, b_ref[...], preferred_element_type=F32).astype(o_ref.dtype)


def _mm(a, b, *, out_dtype, tm, rms_gain=None):
    m, k = a.shape
    _, n = b.shape
    tn = n if n <= 1024 else 1024
    rms = rms_gain is not None
    in_specs = [pl.BlockSpec((tm, k), lambda j, i: (i, 0))]
    args = [a]
    if rms:
        in_specs.append(pl.BlockSpec((1, k), lambda j, i: (0, 0)))
        args.append(rms_gain.reshape(1, k).astype(F32))
    in_specs.append(pl.BlockSpec((k, tn), lambda j, i: (0, j)))
    args.append(b)
    vmem = 2 * (tm * k * a.dtype.itemsize + k * tn * 2 + tm * tn * jnp.dtype(out_dtype).itemsize)
    return pl.pallas_call(
        functools.partial(_mm_kernel, rms=rms),
        grid=(n // tn, m // tm),
        in_specs=in_specs,
        out_specs=pl.BlockSpec((tm, tn), lambda j, i: (i, j)),
        out_shape=jax.ShapeDtypeStruct((m, n), out_dtype),
        compiler_params=_cparams(("parallel", "parallel"), vmem),
        name="mm",
    )(*args)


NA_QB = 2 * GRID_W
NA_SLAB_ROWS = NA_WIN_H + 1
NA_SLAB = NA_SLAB_ROWS * GRID_W


def _natten_tables(rpb, n_rows):
    heads = rpb.shape[0]
    deltas = (0, -2, -4, -5, -7)
    r0s = (0, 2, 4, n_rows - 4, n_rows - 2)
    sentinel = (2 * NA_WIN_H - 1) * (2 * NA_WIN_W - 1)
    idx = np.full((6, NA_QB, NA_SLAB), sentinel, np.int32)
    for case, (delta, r0) in enumerate(zip(deltas, r0s)):
        s0 = r0 + delta
        for ql in range(NA_QB):
            qr, qc = r0 + ql // GRID_W, ql % GRID_W
            sr = min(max(qr - NA_WIN_H // 2, 0), n_rows - NA_WIN_H)
            sc = min(max(qc - NA_WIN_W // 2, 0), GRID_W - NA_WIN_W)
            for kl_r in range(NA_SLAB_ROWS):
                kr = s0 + kl_r
                if not (sr <= kr < sr + NA_WIN_H):
                    continue
                for kc in range(sc, sc + NA_WIN_W):
                    dr = kr - qr + NA_WIN_H - 1
                    dc = kc - qc + NA_WIN_W - 1
                    idx[case, ql, kl_r * GRID_W + kc] = dr * (2 * NA_WIN_W - 1) + dc
    flat = jnp.concatenate([rpb.reshape(heads, -1).astype(F32),
                            jnp.full((heads, 1), NEG_BIG, F32)], axis=1)
    return flat[:, idx]


def _natten_kernel(q_ref, k_ref, v_ref, tab_ref, o_ref, *, n_ctx, n_rows, scale):
    n_ctx_blocks = n_ctx // NA_QB
    n_blocks = q_ref.shape[0] // NA_QB
    kc = k_ref[0:n_ctx, :]
    vc = v_ref[0:n_ctx, :]

    def body(b, carry):
        r0 = 2 * (b - n_ctx_blocks)
        s0 = jnp.minimum(jnp.clip(r0 - NA_WIN_H // 2, 0, n_rows - NA_WIN_H), n_rows - NA_SLAB_ROWS)
        delta = s0 - r0
        case = jnp.where(b < n_ctx_blocks, 5,
                         jnp.where(delta == 0, 0, jnp.where(delta == -2, 1, jnp.where(
                             delta == -4, 2, jnp.where(delta == -5, 3, 4)))))
        row0 = pl.multiple_of(n_ctx + jnp.where(b < n_ctx_blocks, 0, s0) * GRID_W, GRID_W)
        q0 = pl.multiple_of(b * NA_QB, NA_QB)
        q = q_ref[pl.ds(q0, NA_QB), :]
        ks = k_ref[pl.ds(row0, NA_SLAB), :]
        vs = v_ref[pl.ds(row0, NA_SLAB), :]
        s_loc = lax.dot_general(q, ks, NT_DIMS, preferred_element_type=F32) * scale + tab_ref[0, case]
        s_ctx = lax.dot_general(q, kc, NT_DIMS, preferred_element_type=F32) * scale
        m = jnp.maximum(jnp.max(s_loc, axis=-1, keepdims=True), jnp.max(s_ctx, axis=-1, keepdims=True))
        p_loc = jnp.exp(s_loc - m)
        p_ctx = jnp.exp(s_ctx - m)
        l = jnp.sum(p_loc, axis=-1, keepdims=True) + jnp.sum(p_ctx, axis=-1, keepdims=True)
        o = (jnp.dot(p_loc.astype(BF16), vs, preferred_element_type=F32)
             + jnp.dot(p_ctx.astype(BF16), vc, preferred_element_type=F32))
        o_ref[pl.ds(q0, NA_QB), :] = (o / l).astype(o_ref.dtype)
        return carry

    lax.fori_loop(0, n_blocks, body, 0)


def _natten(qkv, tables, *, n_ctx, n_rows):
    t = qkv.shape[0]
    heads, dh = NA_HEADS, LANES
    col = lambda off: pl.BlockSpec((t, dh), lambda h: (0, off + h))
    return pl.pallas_call(
        functools.partial(_natten_kernel, n_ctx=n_ctx, n_rows=n_rows, scale=dh ** -0.5),
        grid=(heads,),
        in_specs=[col(0), col(heads), col(2 * heads),
                  pl.BlockSpec((1, 6, NA_QB, NA_SLAB), lambda h: (h, 0, 0, 0))],
        out_specs=pl.BlockSpec((t, dh), lambda h: (0, h)),
        out_shape=jax.ShapeDtypeStruct((t, heads * dh), BF16),
        compiler_params=_cparams(("parallel",), 2 * (4 * t * dh * 2 + 6 * NA_QB * NA_SLAB * 4)),
        name="natten",
    )(qkv, qkv, qkv, tables)


def _flash_kernel(*refs, groups, dq, dv, n_ctx, tk, has_pe):
    if has_pe:
        q_ref, k_ref, kpe_ref, v_ref, o_ref, acc_ref = refs
    else:
        q_ref, k_ref, v_ref, o_ref, acc_ref = refs
        kpe_ref = None
    tq = q_ref.shape[0]
    t = k_ref.shape[0]
    qi = pl.program_id(1)
    n_main = jnp.where(qi * tq < n_ctx, 0, (t - n_ctx) // tk)

    def k_tile(start, size):
        kt = k_ref[pl.ds(start, size), :]
        if has_pe:
            kt = jnp.concatenate([kt, kpe_ref[pl.ds(start, size), :]], axis=1)
        return kt

    for g in range(groups):
        q = q_ref[:, g * dq:(g + 1) * dq]
        s_t = lax.dot_general(k_tile(0, n_ctx), q, NT_DIMS, preferred_element_type=F32)
        m0 = jnp.max(s_t, axis=0, keepdims=True)
        p = jnp.exp2(s_t - m0)
        l0 = jnp.sum(p, axis=0, keepdims=True)
        acc_ref[...] = lax.dot_general(v_ref[0:n_ctx, :], p.astype(BF16), TN_DIMS,
                                       preferred_element_type=F32)

        def body(j, carry):
            m, l = carry
            start = pl.multiple_of(n_ctx + j * tk, MXU_DIM)
            s_t = lax.dot_general(k_tile(start, tk), q, NT_DIMS, preferred_element_type=F32)
            m_new = jnp.maximum(m, jnp.max(s_t, axis=0, keepdims=True))
            alpha = jnp.exp2(m - m_new)
            p = jnp.exp2(s_t - m_new)
            l_new = alpha * l + jnp.sum(p, axis=0, keepdims=True)
            acc_ref[...] = alpha * acc_ref[...] + lax.dot_general(
                v_ref[pl.ds(start, tk), :], p.astype(BF16), TN_DIMS, preferred_element_type=F32)
            return m_new, l_new

        _, l = lax.fori_loop(0, n_main, body, (m0, l0))
        o_t = acc_ref[...] * (1.0 / l)
        o_ref[:, g * dv:(g + 1) * dv] = o_t.T.astype(o_ref.dtype)


def _flash(q, k, v, kpe, *, n_kv_heads, groups, dq, k_col, v_col, n_ctx, tq, tk):
    t = q.shape[0]
    dv = LANES
    has_pe = kpe is not None
    in_specs = [pl.BlockSpec((tq, groups * dq), lambda h, i: (i, h)),
                pl.BlockSpec((t, LANES), lambda h, i: (0, k_col(h)))]
    args = [q, k]
    if has_pe:
        in_specs.append(pl.BlockSpec((t, LANES), lambda h, i: (0, 0)))
        args.append(kpe)
    in_specs.append(pl.BlockSpec((t, dv), lambda h, i: (0, v_col(h))))
    args.append(v)
    vmem = 2 * (tq * groups * (dq + dv) * 2 + (3 if has_pe else 2) * t * LANES * 2) + 8 * tk * tq * 4
    return pl.pallas_call(
        functools.partial(_flash_kernel, groups=groups, dq=dq, dv=dv, n_ctx=n_ctx, tk=tk, has_pe=has_pe),
        grid=(n_kv_heads, t // tq),
        in_specs=in_specs,
        out_specs=pl.BlockSpec((tq, groups * dv), lambda h, i: (i, h)),
        out_shape=jax.ShapeDtypeStruct((t, n_kv_heads * groups * dv), BF16),
        scratch_shapes=[pltpu.VMEM((dv, tq), F32)],
        compiler_params=_cparams(("parallel", "parallel"), vmem),
        name="flash",
    )(*args)


def _take_top(s, rowid, count):
    sentinel = s.shape[0]
    cur = s
    out = []
    for _ in range(count):
        m = jnp.max(cur, axis=0, keepdims=True)
        first = jnp.min(jnp.where(cur == m, rowid, sentinel), axis=0, keepdims=True)
        cur = jnp.where(rowid == first, -jnp.inf, cur)
        out.append(m)
    return out


_PEER_PAIRS = tuple((i, j) for i in range(PEER_TOPK) for j in range(PEER_TOPK) if (i + 1) * (j + 1) <= PEER_TOPK)


def _route_kernel(q_ref, sk_ref, s1_ref, s2_ref, e1_ref, e2_ref, tau_ref, cand_scr):
    tt = q_ref.shape[0]
    nk = sk_ref.shape[2]
    rowid = lax.broadcasted_iota(jnp.int32, (nk, tt), 0)
    n_pairs = len(_PEER_PAIRS)
    n_pad = -n_pairs % 8
    rowid_c = lax.broadcasted_iota(jnp.int32, (n_pairs + n_pad, tt), 0)

    def body(p, carry):
        def scores(hf):
            c0 = pl.multiple_of((2 * p + hf) * PEER_HALF, PEER_HALF)
            qs = q_ref[:, pl.ds(c0, PEER_HALF)].astype(BF16)
            keys = sk_ref[p, hf].astype(BF16)
            return lax.dot_general(keys, qs, NT_DIMS, preferred_element_type=F32)

        s1 = scores(0)
        s2 = scores(1)
        t1 = _take_top(s1, rowid, PEER_TOPK)
        t2 = _take_top(s2, rowid, PEER_TOPK)
        for r, (i, j) in enumerate(_PEER_PAIRS):
            cand_scr[r:r + 1, :] = t1[i] + t2[j]
        for r in range(n_pairs, n_pairs + n_pad):
            cand_scr[r:r + 1, :] = jnp.full((1, tt), -jnp.inf, F32)
        cand = cand_scr[...]
        tau = _take_top(cand, rowid_c, PEER_TOPK)[-1]
        z = jnp.sum(jnp.where(cand >= tau, jnp.exp(cand - (t1[0] + t2[0])), 0.0), axis=0, keepdims=True)
        s1_ref[p] = s1
        s2_ref[p] = s2
        e1_ref[p] = jnp.exp(s1 - t1[0]) / z
        e2_ref[p] = jnp.exp(s2 - t2[0])
        tau_ref[p] = tau
        return carry

    lax.fori_loop(0, PEER_HEADS, body, 0)


def _peer_route(q, sub_keys, *, tt):
    t = q.shape[0]
    heads, _, nk, half = sub_keys.shape
    big = pl.BlockSpec((heads, nk, tt), lambda i: (0, 0, i))
    big_shape = jax.ShapeDtypeStruct((heads, nk, t), F32)
    return pl.pallas_call(
        _route_kernel,
        grid=(t // tt,),
        in_specs=[pl.BlockSpec((tt, q.shape[1]), lambda i: (i, 0)),
                  pl.BlockSpec(sub_keys.shape, lambda i: (0, 0, 0, 0))],
        out_specs=[big, big, big, big, pl.BlockSpec((heads, 1, tt), lambda i: (0, 0, i))],
        out_shape=[big_shape, big_shape, big_shape, big_shape, jax.ShapeDtypeStruct((heads, 1, t), F32)],
        scratch_shapes=[pltpu.VMEM((len(_PEER_PAIRS) + (-len(_PEER_PAIRS) % 8), tt), F32)],
        compiler_params=_cparams(("parallel",), 2 * (tt * q.shape[1] * 4 + 4 * heads * nk * tt * 4)),
        name="peer_route",
    )(q, sub_keys)


PEER_K1_TILE = 8


def _peer_dense_kernel(h_ref, u_ref, v_ref, s1_ref, e1_ref, s2_ref, e2_ref, tau_ref, o_ref, a_scr, w_scr):
    j = pl.program_id(1)
    tt = h_ref.shape[0]
    nk = s2_ref.shape[1]
    heads = s2_ref.shape[0]
    a_scr[...] = lax.dot_general(u_ref[...], h_ref[...], NT_DIMS, preferred_element_type=F32)

    def body(a, carry):
        r0 = pl.multiple_of(a * nk, 16)
        for lc in range(tt // LANES):
            ls = slice(lc * LANES, (lc + 1) * LANES)
            gates = jnp.zeros((nk, LANES), F32)
            for p in range(heads):
                c = s1_ref[p, a, :, ls] + s2_ref[p, :, ls]
                w = e1_ref[p, a, :, ls] * e2_ref[p, :, ls]
                gates = gates + jnp.where(c >= tau_ref[p, :, ls], w, 0.0)
            act = a_scr[pl.ds(r0, nk), ls]
            gelu = 0.5 * act * (1.0 + lax.erf(act * (2.0 ** -0.5)))
            w_scr[pl.ds(r0, nk), ls] = (gelu * gates).astype(w_scr.dtype)
        return carry

    lax.fori_loop(0, PEER_K1_TILE, body, 0)
    y = lax.dot_general(w_scr[...], v_ref[...], TN_DIMS, preferred_element_type=F32)

    @pl.when(j == 0)
    def _():
        o_ref[...] = y

    @pl.when(j > 0)
    def _():
        o_ref[...] += y


def _peer_dense(h, u, v, s1, s2, e1, e2, tau, *, tt):
    t, d = h.shape
    heads, nk, _ = s1.shape
    te = PEER_K1_TILE * nk
    n_tiles = u.shape[0] // te
    once = pl.Buffered(1)
    tok = pl.BlockSpec((heads, nk, tt), lambda i, j: (0, 0, i), pipeline_mode=once)
    k1 = pl.BlockSpec((heads, PEER_K1_TILE, 1, tt), lambda i, j: (0, j, 0, i))
    s1 = s1.reshape(heads, nk, 1, t)
    e1 = e1.reshape(heads, nk, 1, t)
    vmem = (tt * d * 2 + tt * d * 4 + 2 * heads * nk * tt * 4 + 4 * te * d * 2
            + te * tt * 6 + 4 * heads * PEER_K1_TILE * 8 * tt * 4)
    return pl.pallas_call(
        _peer_dense_kernel,
        grid=(t // tt, n_tiles),
        in_specs=[pl.BlockSpec((tt, d), lambda i, j: (i, 0), pipeline_mode=once),
                  pl.BlockSpec((te, d), lambda i, j: (j, 0)),
                  pl.BlockSpec((te, d), lambda i, j: (j, 0)),
                  k1, k1, tok, tok,
                  pl.BlockSpec((heads, 1, tt), lambda i, j: (0, 0, i))],
        out_specs=pl.BlockSpec((tt, d), lambda i, j: (i, 0), pipeline_mode=once),
        out_shape=jax.ShapeDtypeStruct((t, d), F32),
        scratch_shapes=[pltpu.VMEM((te, tt), F32), pltpu.VMEM((te, tt), BF16)],
        compiler_params=_cparams(("parallel", "arbitrary"), vmem),
        name="peer_dense",
    )(h, u, v, s1, e1, s2, e2, tau)


def _rope_tables(n_ctx, seq, dim):
    quarter = dim // 4
    pos = jnp.arange(seq, dtype=jnp.int32)
    rows, cols = (pos // GRID_W).astype(F32), (pos % GRID_W).astype(F32)
    inv = ROPE_THETA ** (-jnp.arange(quarter, dtype=F32) / quarter)
    parts_c, parts_s = [], []
    for p in (rows, cols):
        ang = p[:, None] * inv[None, :]
        c, s = jnp.cos(ang), jnp.sin(ang)
        parts_c += [c, c]
        parts_s += [-s, s]
    cos = jnp.concatenate(parts_c, axis=1)
    sin = jnp.concatenate(parts_s, axis=1)
    reps = LANES // dim
    cos, sin = jnp.tile(cos, (1, reps)), jnp.tile(sin, (1, reps))
    cos = jnp.concatenate([jnp.ones((n_ctx, LANES), F32), cos], axis=0)
    sin = jnp.concatenate([jnp.zeros((n_ctx, LANES), F32), sin], axis=0)
    return cos, sin


def kernel(x, c, ctx, c_ctx, ada_w, ada_b, ln_g, ln_b, na_w_qkv, na_rpb, na_w_o, mla_w_dq, mla_q_norm, mla_w_uq, mla_w_dkv, mla_kv_norm, mla_w_ukv, mla_w_o, gqa_w_q, gqa_w_k, gqa_w_v, gqa_q_norm, gqa_k_norm, gqa_w_o, peer_w_query, peer_sub_keys, peer_u, peer_v):
    batch, seq, d = x.shape
    n_ctx = ctx.shape[1]
    depth = ada_w.shape[0]
    assert batch == 1 and seq % GRID_W == 0 and n_ctx % MXU_DIM == 0
    t = n_ctx + seq
    n_rows = seq // GRID_W
    alpha = (2 * depth) ** 0.25

    tm_row = 128
    tm_mm = _largest_divisor(t, (768, 512, 256))
    tq, tk = 256, 512
    tt_route = 128
    tt_dense = _largest_divisor(t, (384, 256))
    assert seq % tk == 0 and n_ctx % tq == 0 and n_rows >= 2 * NA_WIN_H
    ncb = n_ctx // tm_row

    xs = jnp.concatenate([ctx[0], x[0]], axis=0)
    cmat = jnp.zeros((8, d), F32).at[0].set(c_ctx).at[1].set(c[0])
    mod = _adaln(cmat, ada_w, ada_b)[:, :2].reshape(depth, 2, N_MOD, 1, d)
    mods = lambda i, k: mod[i, :, k]

    cos128, sin128 = _rope_tables(n_ctx, seq, GQA_HEAD_DIM)
    cos64, sin64 = _rope_tables(n_ctx, seq, MLA_ROPE)

    h = _modulate(xs, mods(0, 1), mods(0, 0), n_ctx_blocks=ncb, tm=tm_row)
    for i in range(depth):
        kind, j = i % N_MIXERS, i // N_MIXERS
        if kind == 0:
            qkv = _mm(h, na_w_qkv[j].astype(BF16), out_dtype=BF16, tm=tm_mm)
            o = _natten(qkv, _natten_tables(na_rpb[j], n_rows), n_ctx=n_ctx, n_rows=n_rows)
            y = _mm(o, na_w_o[j].astype(BF16), out_dtype=F32, tm=tm_mm)
        elif kind == 1:
            qk_scale = (MLA_NOPE + MLA_ROPE) ** -0.5 * LOG2E
            rank = mla_w_uq.shape[1]
            w_uq = mla_w_uq[j].reshape(rank, MLA_HEADS, MLA_NOPE + MLA_ROPE)
            w_uq = jnp.pad(w_uq, ((0, 0), (0, 0), (0, MXU_DIM - MLA_NOPE - MLA_ROPE)))
            w_uq = w_uq.reshape(rank, MLA_HEADS * MXU_DIM).astype(BF16)
            w_pe = mla_w_dkv[j][:, MLA_KV_RANK:]
            w_dkv = jnp.concatenate([mla_w_dkv[j][:, :MLA_KV_RANK], w_pe, w_pe], axis=1).astype(BF16)
            cq = _mm(h, mla_w_dq[j].astype(BF16), out_dtype=F32, tm=tm_mm)
            q_raw = _mm(cq, w_uq, out_dtype=F32, tm=tm_mm, rms_gain=mla_q_norm[j])
            q = _rope_rows(q_raw, cos64, sin64, None, modes=("scale", "rope"), half=MLA_ROPE // 4,
                           scale=qk_scale, tm=tm_row)
            ckv = _mm(h, w_dkv, out_dtype=F32, tm=tm_mm)
            kv = _mm(ckv[:, :MLA_KV_RANK], mla_w_ukv[j].astype(BF16), out_dtype=BF16, tm=tm_mm,
                     rms_gain=mla_kv_norm[j])
            kpe = _rope_rows(ckv[:, MLA_KV_RANK:], cos64, sin64, None, modes=("rope",),
                             half=MLA_ROPE // 4, scale=1.0, tm=tm_row)
            o = _flash(q, kv, kv, kpe, n_kv_heads=MLA_HEADS, groups=1, dq=MXU_DIM,
                       k_col=lambda hh: 2 * hh, v_col=lambda hh: 2 * hh + 1, n_ctx=n_ctx, tq=tq, tk=tk)
            y = _mm(o, mla_w_o[j].astype(BF16), out_dtype=F32, tm=tm_mm)
        else:
            qk_scale = GQA_HEAD_DIM ** -0.5 * LOG2E
            q_raw = _mm(h, gqa_w_q[j].astype(BF16), out_dtype=F32, tm=tm_mm)
            k_raw = _mm(h, gqa_w_k[j].astype(BF16), out_dtype=F32, tm=tm_mm)
            vv = _mm(h, gqa_w_v[j].astype(BF16), out_dtype=BF16, tm=tm_mm)
            q = _rope_rows(q_raw, cos128, sin128, gqa_q_norm[j], modes=("normrope",),
                           half=GQA_HEAD_DIM // 4, scale=qk_scale, tm=tm_row)
            kk = _rope_rows(k_raw, cos128, sin128, gqa_k_norm[j], modes=("normrope",),
                            half=GQA_HEAD_DIM // 4, scale=1.0, tm=tm_row)
            o = _flash(q, kk, vv, None, n_kv_heads=GQA_KV_HEADS, groups=GQA_HEADS // GQA_KV_HEADS,
                       dq=GQA_HEAD_DIM, k_col=lambda hh: hh, v_col=lambda hh: hh, n_ctx=n_ctx, tq=tq, tk=tk)
            y = _mm(o, gqa_w_o[j].astype(BF16), out_dtype=F32, tm=tm_mm)

        xs, h2 = _ln_mod(xs, y, mods(i, 2), ln_g[i, 0], ln_b[i, 0], mods(i, 4), mods(i, 3),
                         alpha=alpha, n_ctx_blocks=ncb, tm=tm_row)
        pq = _mm(h2, peer_w_query[i].astype(BF16), out_dtype=F32, tm=tm_mm)
        s1, s2, e1, e2, tau = _peer_route(pq, peer_sub_keys[i], tt=tt_route)
        f = _peer_dense(h2, peer_u[i].astype(BF16), peer_v[i].astype(BF16), s1, s2, e1, e2, tau, tt=tt_dense)
        if i + 1 < depth:
            xs, h = _ln_mod(xs, f, mods(i, 5), ln_g[i, 1], ln_b[i, 1], mods(i + 1, 1), mods(i + 1, 0),
                            alpha=alpha, n_ctx_blocks=ncb, tm=tm_row)
        else:
            xs, _ = _ln_mod(xs, f, mods(i, 5), ln_g[i, 1], ln_b[i, 1], None, None,
                            alpha=alpha, n_ctx_blocks=ncb, tm=tm_row)
    return xs[n_ctx:][None]
```

```python
import functools
import math

import numpy as np
import jax
import jax.numpy as jnp
from jax import lax
from jax.experimental import pallas as pl
from jax.experimental.pallas import tpu as pltpu

F32 = jnp.float32
BF16 = jnp.bfloat16

GRID_W = 64
ROPE_THETA = 10000.0
RMS_EPS = 1e-6
LN_EPS = 1e-5
N_MOD = 6
NA_HEADS = 32
NA_WIN_H = 8
NA_WIN_W = 16
MLA_HEADS = 32
MLA_NOPE = 128
MLA_ROPE = 64
MLA_V = 128
MLA_KV_RANK = 512
GQA_HEADS = 32
GQA_KV_HEADS = 8
GQA_HEAD_DIM = 128
PEER_HEADS = 8
PEER_N_KEYS = 96
PEER_TOPK = 16
PEER_HALF = 128
N_MIXERS = 3

LANES = 128
MXU_DIM = 256
VMEM_LIMIT_CAP = 58 * 1024 * 1024

NEG_BIG = -1e30
LOG2E = 1.4426950408889634

NT_DIMS = (((1,), (1,)), ((), ()))
TN_DIMS = (((0,), (0,)), ((), ()))


def _largest_divisor(n, candidates):
    return next(c for c in candidates if n % c == 0)


def _cparams(sem, vmem_bytes, flags=None):
    limit = int(min(VMEM_LIMIT_CAP, max(32 * 1024 * 1024, vmem_bytes + (8 << 20))))
    return pltpu.CompilerParams(dimension_semantics=sem, vmem_limit_bytes=limit, flags=flags)


def _adaln_kernel(c_ref, w_ref, b_ref, o_ref):
    c = c_ref[...]
    cond = c * jax.nn.sigmoid(c)
    o_ref[0] = jnp.dot(cond, w_ref[0], preferred_element_type=F32,
                       precision=lax.Precision.HIGHEST) + b_ref[0]


def _adaln(cmat, ada_w, ada_b):
    depth, d, n = ada_w.shape
    tn = 512
    return pl.pallas_call(
        _adaln_kernel,
        grid=(depth, n // tn),
        in_specs=[
            pl.BlockSpec((8, d), lambda i, j: (0, 0)),
            pl.BlockSpec((1, d, tn), lambda i, j: (i, 0, j)),
            pl.BlockSpec((1, 1, tn), lambda i, j: (i, 0, j)),
        ],
        out_specs=pl.BlockSpec((1, 8, tn), lambda i, j: (i, 0, j)),
        out_shape=jax.ShapeDtypeStruct((depth, 8, n), F32),
        compiler_params=_cparams(("parallel", "parallel"), 2 * d * tn * 4),
        name="adaln",
    )(cmat, ada_w, ada_b.reshape(depth, 1, n))


def _modulate_kernel(x_ref, sc_ref, sh_ref, h_ref):
    h_ref[...] = (x_ref[...] * (1.0 + sc_ref[0]) + sh_ref[0]).astype(h_ref.dtype)


def _modulate(x, sc, sh, *, n_ctx_blocks, tm):
    t, d = x.shape
    kind = lambda i: (jnp.where(i < n_ctx_blocks, 0, 1), 0, 0)
    return pl.pallas_call(
        _modulate_kernel,
        grid=(t // tm,),
        in_specs=[pl.BlockSpec((tm, d), lambda i: (i, 0)),
                  pl.BlockSpec((1, 1, d), kind), pl.BlockSpec((1, 1, d), kind)],
        out_specs=pl.BlockSpec((tm, d), lambda i: (i, 0)),
        out_shape=jax.ShapeDtypeStruct((t, d), BF16),
        compiler_params=_cparams(("parallel",), 2 * tm * d * 6),
        name="modulate",
    )(x, sc, sh)


def _ln_mod_kernel(*refs, alpha, with_h):
    if with_h:
        x_ref, y_ref, g_ref, lng_ref, lnb_ref, sc_ref, sh_ref, xo_ref, ho_ref = refs
    else:
        x_ref, y_ref, g_ref, lng_ref, lnb_ref, xo_ref = refs
    v = alpha * x_ref[...] + g_ref[0] * y_ref[...].astype(F32)
    mu = jnp.mean(v, axis=-1, keepdims=True)
    vc = v - mu
    var = jnp.mean(vc * vc, axis=-1, keepdims=True)
    xn = vc * lax.rsqrt(var + LN_EPS) * lng_ref[...] + lnb_ref[...]
    xo_ref[...] = xn
    if with_h:
        ho_ref[...] = (xn * (1.0 + sc_ref[0]) + sh_ref[0]).astype(ho_ref.dtype)


def _ln_mod(x, y, gate, lng, lnb, sc, sh, *, alpha, n_ctx_blocks, tm):
    t, d = x.shape
    with_h = sc is not None
    kind = lambda i: (jnp.where(i < n_ctx_blocks, 0, 1), 0, 0)
    row = pl.BlockSpec((tm, d), lambda i: (i, 0))
    vec = pl.BlockSpec((1, d), lambda i: (0, 0))
    in_specs = [row, row, pl.BlockSpec((1, 1, d), kind), vec, vec]
    args = [x, y, gate, lng.reshape(1, d), lnb.reshape(1, d)]
    out_shape = [jax.ShapeDtypeStruct((t, d), F32)]
    out_specs = [row]
    if with_h:
        in_specs += [pl.BlockSpec((1, 1, d), kind), pl.BlockSpec((1, 1, d), kind)]
        args += [sc, sh]
        out_shape.append(jax.ShapeDtypeStruct((t, d), BF16))
        out_specs.append(row)
    res = pl.pallas_call(
        functools.partial(_ln_mod_kernel, alpha=alpha, with_h=with_h),
        grid=(t // tm,),
        in_specs=in_specs, out_specs=out_specs, out_shape=out_shape,
        compiler_params=_cparams(("parallel",), 2 * tm * d * (4 + y.dtype.itemsize + 4 + 2)),
        name="ln_mod",
    )(*args)
    return (res[0], res[1]) if with_h else (res[0], None)


def _rope_rows_kernel(*refs, modes, half, scale, has_gain):
    if has_gain:
        x_ref, cos_ref, sin_ref, g_ref, o_ref = refs
    else:
        x_ref, cos_ref, sin_ref, o_ref = refs
    tm = x_ref.shape[0]
    cos = cos_ref[...]
    sin = sin_ref[...]
    lane = lax.broadcasted_iota(jnp.int32, (tm, LANES), 1)
    first = (lane % (2 * half)) < half
    for j in range(x_ref.shape[1] // LANES):
        mode = modes[j % len(modes)]
        x = x_ref[:, j * LANES:(j + 1) * LANES].astype(F32)
        if mode == "normrope":
            x = x * lax.rsqrt(jnp.mean(x * x, axis=-1, keepdims=True) + RMS_EPS) * g_ref[...]
        if mode in ("rope", "normrope"):
            partner = jnp.where(first, pltpu.roll(x, LANES - half, 1), pltpu.roll(x, half, 1))
            x = x * cos + partner * sin
        o_ref[:, j * LANES:(j + 1) * LANES] = (x * scale).astype(o_ref.dtype)


def _rope_rows(x, cos, sin, gain, *, modes, half, scale, tm):
    t, w = x.shape
    has_gain = gain is not None
    row = lambda width: pl.BlockSpec((tm, width), lambda i: (i, 0))
    in_specs = [row(w), row(LANES), row(LANES)]
    args = [x, cos, sin]
    if has_gain:
        in_specs.append(pl.BlockSpec((1, LANES), lambda i: (0, 0)))
        args.append(gain.reshape(1, LANES).astype(F32))
    return pl.pallas_call(
        functools.partial(_rope_rows_kernel, modes=modes, half=half, scale=scale, has_gain=has_gain),
        grid=(t // tm,),
        in_specs=in_specs, out_specs=row(w),
        out_shape=jax.ShapeDtypeStruct((t, w), BF16),
        compiler_params=_cparams(("parallel",), 2 * tm * w * (x.dtype.itemsize + 2)),
        name="rope_rows",
    )(*args)


def _mm_kernel(*refs, rms):
    if rms:
        a_ref, g_ref, b_ref, o_ref = refs
        x = a_ref[...].astype(F32)
        x = x * lax.rsqrt(jnp.mean(x * x, axis=-1, keepdims=True) + RMS_EPS) * g_ref[...]
        a = x.astype(BF16)
    else:
        a_ref, b_ref, o_ref = refs
        a = a_ref[...]
    o_ref[...] = jnp.dot(a, b_ref[...], preferred_element_type=F32).astype(o_ref.dtype)


def _mm(a, b, *, out_dtype, tm, tn=1024, rms_gain=None):
    m, k = a.shape
    _, n = b.shape
    tn = min(n, tn)
    rms = rms_gain is not None
    in_specs = [pl.BlockSpec((tm, k), lambda j, i: (i, 0))]
    args = [a]
    if rms:
        in_specs.append(pl.BlockSpec((1, k), lambda j, i: (0, 0)))
        args.append(rms_gain.reshape(1, k).astype(F32))
    in_specs.append(pl.BlockSpec((k, tn), lambda j, i: (0, j)))
    args.append(b)
    vmem = 2 * (tm * k * a.dtype.itemsize + k * tn * 2 + tm * tn * jnp.dtype(out_dtype).itemsize)
    return pl.pallas_call(
        functools.partial(_mm_kernel, rms=rms),
        grid=(n // tn, m // tm),
        in_specs=in_specs,
        out_specs=pl.BlockSpec((tm, tn), lambda j, i: (i, j)),
        out_shape=jax.ShapeDtypeStruct((m, n), out_dtype),
        compiler_params=_cparams(("parallel", "parallel"), vmem),
        name="mm",
    )(*args)


NA_QB = 2 * GRID_W
NA_SLAB_ROWS = NA_WIN_H + 1
NA_SLAB = NA_SLAB_ROWS * GRID_W


def _natten_tables(rpb, n_rows):
    heads = rpb.shape[0]
    deltas = (0, -2, -4, -5, -7)
    r0s = (0, 2, 4, n_rows - 4, n_rows - 2)
    shifts = np.zeros((2 * NA_WIN_W - 1, GRID_W, GRID_W), np.float32)
    colmask = np.full((GRID_W, GRID_W), NEG_BIG, np.float32)
    for qc in range(GRID_W):
        sc = min(max(qc - NA_WIN_W // 2, 0), GRID_W - NA_WIN_W)
        for kc in range(sc, sc + NA_WIN_W):
            shifts[kc - qc + NA_WIN_W - 1, qc, kc] = 1.0
            colmask[qc, kc] = 0.0
    toe = jnp.einsum("hrd,dqk->hrqk", rpb.astype(F32), shifts, precision=lax.Precision.HIGHEST) + colmask
    masked = jnp.full((heads, GRID_W, GRID_W), NEG_BIG, F32)
    cases = []
    for delta, r0 in zip(deltas, r0s):
        s0 = r0 + delta
        halves = []
        for qr in (r0, r0 + 1):
            sr = min(max(qr - NA_WIN_H // 2, 0), n_rows - NA_WIN_H)
            assert s0 <= sr and sr + NA_WIN_H <= s0 + NA_SLAB_ROWS
            halves.append(jnp.concatenate(
                [toe[:, kr - qr + NA_WIN_H - 1] if sr <= kr < sr + NA_WIN_H else masked
                 for kr in range(s0, s0 + NA_SLAB_ROWS)], axis=-1))
        cases.append(jnp.concatenate(halves, axis=1))
    cases.append(jnp.full((heads, NA_QB, NA_SLAB), NEG_BIG, F32))
    return jnp.stack(cases, axis=1)


def _natten_kernel(q_ref, k_ref, v_ref, tab_ref, o_ref, *, n_ctx, n_rows, scale):
    n_ctx_blocks = n_ctx // NA_QB
    n_blocks = q_ref.shape[0] // NA_QB
    kc = k_ref[0:n_ctx, :]
    vc = v_ref[0:n_ctx, :]

    def body(b, carry):
        r0 = 2 * (b - n_ctx_blocks)
        s0 = jnp.minimum(jnp.clip(r0 - NA_WIN_H // 2, 0, n_rows - NA_WIN_H), n_rows - NA_SLAB_ROWS)
        delta = s0 - r0
        case = jnp.where(b < n_ctx_blocks, 5,
                         jnp.where(delta == 0, 0, jnp.where(delta == -2, 1, jnp.where(
                             delta == -4, 2, jnp.where(delta == -5, 3, 4)))))
        row0 = pl.multiple_of(n_ctx + jnp.where(b < n_ctx_blocks, 0, s0) * GRID_W, GRID_W)
        q0 = pl.multiple_of(b * NA_QB, NA_QB)
        q = q_ref[pl.ds(q0, NA_QB), :]
        ks = k_ref[pl.ds(row0, NA_SLAB), :]
        vs = v_ref[pl.ds(row0, NA_SLAB), :]
        s_loc = lax.dot_general(q, ks, NT_DIMS, preferred_element_type=F32) * scale + tab_ref[0, case]
        s_ctx = lax.dot_general(q, kc, NT_DIMS, preferred_element_type=F32) * scale
        m = jnp.maximum(jnp.max(s_loc, axis=-1, keepdims=True), jnp.max(s_ctx, axis=-1, keepdims=True))
        p_loc = jnp.exp(s_loc - m)
        p_ctx = jnp.exp(s_ctx - m)
        l = jnp.sum(p_loc, axis=-1, keepdims=True) + jnp.sum(p_ctx, axis=-1, keepdims=True)
        o = (jnp.dot(p_loc.astype(BF16), vs, preferred_element_type=F32)
             + jnp.dot(p_ctx.astype(BF16), vc, preferred_element_type=F32))
        o_ref[pl.ds(q0, NA_QB), :] = (o / l).astype(o_ref.dtype)
        return carry

    lax.fori_loop(0, n_blocks, body, 0, unroll=2)


def _natten(qkv, tables, *, n_ctx, n_rows):
    t = qkv.shape[0]
    heads, dh = NA_HEADS, LANES
    col = lambda off: pl.BlockSpec((t, dh), lambda h: (0, off + h))
    return pl.pallas_call(
        functools.partial(_natten_kernel, n_ctx=n_ctx, n_rows=n_rows, scale=dh ** -0.5),
        grid=(heads,),
        in_specs=[col(0), col(heads), col(2 * heads),
                  pl.BlockSpec((1, 6, NA_QB, NA_SLAB), lambda h: (h, 0, 0, 0))],
        out_specs=pl.BlockSpec((t, dh), lambda h: (0, h)),
        out_shape=jax.ShapeDtypeStruct((t, heads * dh), BF16),
        compiler_params=_cparams(("parallel",), 2 * (4 * t * dh * 2 + 6 * NA_QB * NA_SLAB * 4)),
        name="natten",
    )(qkv, qkv, qkv, tables)


def _flash_kernel(*refs, groups, dq, dv, n_ctx, tk, k_off, v_off, has_pe, shared_kv):
    refs = list(refs)
    q_ref = refs.pop(0)
    k_ref = refs.pop(0)
    kpe_ref = refs.pop(0) if has_pe else None
    v_ref = k_ref if shared_kv else refs.pop(0)
    o_ref, acc_ref, sa_ref, sb_ref = refs
    tq = q_ref.shape[0]
    t = k_ref.shape[0]
    qi = pl.program_id(1)
    n_main = jnp.where(qi * tq < n_ctx, 0, (t - n_ctx) // tk)

    def scores(g, start, size):
        kt = k_ref[pl.ds(start, size), k_off[g]:k_off[g] + LANES]
        if has_pe:
            kt = jnp.concatenate([kt, kpe_ref[pl.ds(start, size), :]], axis=1)
        return lax.dot_general(kt, q_ref[:, g * dq:(g + 1) * dq], NT_DIMS, preferred_element_type=F32)

    def pv(g, start, size, p):
        vt = v_ref[pl.ds(start, size), v_off[g]:v_off[g] + dv]
        return lax.dot_general(vt, p.astype(BF16), TN_DIMS, preferred_element_type=F32)

    carry = []
    for g in range(groups):
        s_t = scores(g, 0, n_ctx)
        m0 = jnp.max(s_t, axis=0, keepdims=True)
        p = jnp.exp2(s_t - m0)
        carry += [m0, jnp.sum(p, axis=0, keepdims=True)]
        acc_ref[g] = pv(g, 0, n_ctx, p)
        sa_ref[g] = scores(g, n_ctx, tk)

    def process(s_ref, start, carry):
        out = []
        for g in range(groups):
            m, l = carry[2 * g], carry[2 * g + 1]
            s_t = s_ref[g]
            m_new = jnp.maximum(m, jnp.max(s_t, axis=0, keepdims=True))
            alpha = jnp.exp2(m - m_new)
            p = jnp.exp2(s_t - m_new)
            out += [m_new, alpha * l + jnp.sum(p, axis=0, keepdims=True)]
            acc_ref[g] = alpha * acc_ref[g] + pv(g, start, tk, p)
        return tuple(out)

    def body(j, carry):
        t0 = pl.multiple_of(n_ctx + 2 * j * tk, MXU_DIM)
        t1 = pl.multiple_of(t0 + tk, MXU_DIM)
        t2 = pl.multiple_of(jnp.minimum(t1 + tk, t - tk), MXU_DIM)
        for g in range(groups):
            sb_ref[g] = scores(g, t1, tk)
        carry = process(sa_ref, t0, carry)
        for g in range(groups):
            sa_ref[g] = scores(g, t2, tk)
        return process(sb_ref, t1, carry)

    carry = lax.fori_loop(0, n_main // 2, body, tuple(carry))
    for g in range(groups):
        o_t = acc_ref[g] * (1.0 / carry[2 * g + 1])
        o_ref[:, g * dv:(g + 1) * dv] = o_t.T.astype(o_ref.dtype)


def _flash(q, k, v, kpe, *, n_steps, groups, dq, kv_width, k_off, v_off, n_ctx, tq, tk):
    t = q.shape[0]
    dv = LANES
    has_pe = kpe is not None
    shared_kv = v is None
    once = pl.Buffered(1)
    kv_spec = pl.BlockSpec((t, kv_width), lambda h, i: (0, h), pipeline_mode=once)
    in_specs = [pl.BlockSpec((tq, groups * dq), lambda h, i: (i, h)), kv_spec]
    args = [q, k]
    if has_pe:
        in_specs.append(pl.BlockSpec((t, LANES), lambda h, i: (0, 0), pipeline_mode=once))
        args.append(kpe)
    if not shared_kv:
        in_specs.append(kv_spec)
        args.append(v)
    assert ((t - n_ctx) // tk) % 2 == 0
    vmem = (2 * tq * groups * (dq + dv) * 2 + (1 if shared_kv else 2) * t * kv_width * 2
            + t * LANES * 2 + groups * (dv * tq * 4 + 4 * tk * tq * 4))
    return pl.pallas_call(
        functools.partial(_flash_kernel, groups=groups, dq=dq, dv=dv, n_ctx=n_ctx, tk=tk, k_off=k_off,
                          v_off=v_off, has_pe=has_pe, shared_kv=shared_kv),
        grid=(n_steps, t // tq),
        in_specs=in_specs,
        out_specs=pl.BlockSpec((tq, groups * dv), lambda h, i: (i, h)),
        out_shape=jax.ShapeDtypeStruct((t, n_steps * groups * dv), BF16),
        scratch_shapes=[pltpu.VMEM((groups, dv, tq), F32), pltpu.VMEM((groups, tk, tq), F32),
                        pltpu.VMEM((groups, tk, tq), F32)],
        compiler_params=_cparams(("parallel", "parallel"), vmem),
        name="flash",
    )(*args)


def _take_top(s, rowid, count):
    sentinel = s.shape[0]
    cur = s
    out = []
    for _ in range(count):
        m = jnp.max(cur, axis=0, keepdims=True)
        first = jnp.min(jnp.where(cur == m, rowid, sentinel), axis=0, keepdims=True)
        cur = jnp.where(rowid == first, -jnp.inf, cur)
        out.append(m)
    return out


_PEER_PAIRS = tuple((i, j) for i in range(PEER_TOPK) for j in range(PEER_TOPK) if (i + 1) * (j + 1) <= PEER_TOPK)


def _route_kernel(q_ref, sk_ref, s1_ref, s2_ref, e1_ref, e2_ref, tau_ref, cand_scr):
    tt = q_ref.shape[0]
    nk = sk_ref.shape[2]
    rowid = lax.broadcasted_iota(jnp.int32, (nk, tt), 0)
    n_pairs = len(_PEER_PAIRS)
    n_pad = -n_pairs % 8
    rowid_c = lax.broadcasted_iota(jnp.int32, (n_pairs + n_pad, tt), 0)

    def body(p, carry):
        def scores(hf):
            c0 = pl.multiple_of((2 * p + hf) * PEER_HALF, PEER_HALF)
            qs = q_ref[:, pl.ds(c0, PEER_HALF)].astype(BF16)
            keys = sk_ref[p, hf].astype(BF16)
            return lax.dot_general(keys, qs, NT_DIMS, preferred_element_type=F32)

        s1 = scores(0)
        s2 = scores(1)
        t1 = _take_top(s1, rowid, PEER_TOPK)
        t2 = _take_top(s2, rowid, PEER_TOPK)
        for r, (i, j) in enumerate(_PEER_PAIRS):
            cand_scr[r:r + 1, :] = t1[i] + t2[j]
        for r in range(n_pairs, n_pairs + n_pad):
            cand_scr[r:r + 1, :] = jnp.full((1, tt), -jnp.inf, F32)
        cand = cand_scr[...]
        tau = _take_top(cand, rowid_c, PEER_TOPK)[-1]
        z = jnp.sum(jnp.where(cand >= tau, jnp.exp(cand - (t1[0] + t2[0])), 0.0), axis=0, keepdims=True)
        s1_ref[p] = s1
        s2_ref[p] = s2
        e1_ref[p] = jnp.exp(s1 - t1[0]) / z
        e2_ref[p] = jnp.exp(s2 - t2[0])
        tau_ref[p] = tau
        return carry

    lax.fori_loop(0, PEER_HEADS, body, 0)


def _peer_route(q, sub_keys, *, tt):
    t = q.shape[0]
    heads, _, nk, half = sub_keys.shape
    big = pl.BlockSpec((heads, nk, tt), lambda i: (0, 0, i))
    big_shape = jax.ShapeDtypeStruct((heads, nk, t), F32)
    return pl.pallas_call(
        _route_kernel,
        grid=(t // tt,),
        in_specs=[pl.BlockSpec((tt, q.shape[1]), lambda i: (i, 0)),
                  pl.BlockSpec(sub_keys.shape, lambda i: (0, 0, 0, 0))],
        out_specs=[big, big, big, big, pl.BlockSpec((heads, 1, tt), lambda i: (0, 0, i))],
        out_shape=[big_shape, big_shape, big_shape, big_shape, jax.ShapeDtypeStruct((heads, 1, t), F32)],
        scratch_shapes=[pltpu.VMEM((len(_PEER_PAIRS) + (-len(_PEER_PAIRS) % 8), tt), F32)],
        compiler_params=_cparams(("parallel",), 2 * (tt * q.shape[1] * 4 + 4 * heads * nk * tt * 4)),
        name="peer_route",
    )(q, sub_keys)


PEER_K1_TILE = 8


def _peer_gate_kernel(h_ref, u_ref, s1_ref, e1_ref, s2_ref, e2_ref, tau_ref, o_ref, a_even, a_odd):
    s = pl.program_id(0)
    tt = h_ref.shape[0]
    heads, nk, _ = s2_ref.shape

    @pl.when(s == 0)
    def _():
        a_odd[...] = jnp.zeros_like(a_odd)

    def gated(a_read, a, ls):
        gates = jnp.zeros((nk, LANES), F32)
        for p in range(heads):
            c = s1_ref[p, a, :, ls] + s2_ref[p, :, ls]
            w = e1_ref[p, a, :, ls] * e2_ref[p, :, ls]
            gates = gates + jnp.where(c >= tau_ref[p, :, ls], w, 0.0)
        act = a_read[a * nk:(a + 1) * nk, ls]
        gelu = 0.5 * act * (1.0 + lax.erf(act * (2.0 ** -0.5)))
        return gelu * gates

    def step(a_write, a_read):
        def chunk(n, carry):
            n0 = pl.multiple_of(n * MXU_DIM, MXU_DIM)
            a_write[:, pl.ds(n0, MXU_DIM)] = lax.dot_general(
                u_ref[...], h_ref[pl.ds(n0, MXU_DIM), :], NT_DIMS, preferred_element_type=F32)
            for c in range(MXU_DIM // LANES):
                ls = pl.ds(pl.multiple_of(n0 + c * LANES, LANES), LANES)
                cols = [gated(a_read, a, ls) for a in range(PEER_K1_TILE)]
                o_ref[ls, :] = jnp.concatenate(cols, axis=0).T.astype(o_ref.dtype)
            return carry

        lax.fori_loop(0, tt // MXU_DIM, chunk, 0)

    @pl.when(s % 2 == 0)
    def _():
        step(a_even, a_odd)

    @pl.when(s % 2 == 1)
    def _():
        step(a_odd, a_even)


def _peer_gate(h, u, s1, s2, e1, e2, tau, *, tt):
    t, d = h.shape
    heads, nk, _ = s1.shape
    te = PEER_K1_TILE * nk
    ne = u.shape[0] // te
    n_pairs = (t // tt) * ne
    cur = lambda s: jnp.minimum(s, n_pairs - 1)
    prev = lambda s: jnp.maximum(s - 1, 0)
    tok = pl.BlockSpec((heads, nk, tt), lambda s: (0, 0, prev(s) // ne))
    k1 = pl.BlockSpec((heads, PEER_K1_TILE, 1, tt), lambda s: (0, prev(s) % ne, 0, prev(s) // ne))
    s1 = s1.reshape(heads, nk, 1, t)
    e1 = e1.reshape(heads, nk, 1, t)
    vmem = (2 * tt * d * 2 + 2 * te * d * 2 + 2 * te * tt * 4 + 2 * tt * te * 2
            + 4 * heads * nk * tt * 4 + 4 * heads * PEER_K1_TILE * 8 * tt * 4)
    return pl.pallas_call(
        _peer_gate_kernel,
        grid=(n_pairs + 1,),
        in_specs=[pl.BlockSpec((tt, d), lambda s: (cur(s) // ne, 0)),
                  pl.BlockSpec((te, d), lambda s: (cur(s) % ne, 0)),
                  k1, k1, tok, tok,
                  pl.BlockSpec((heads, 1, tt), lambda s: (0, 0, prev(s) // ne))],
        out_specs=pl.BlockSpec((tt, te), lambda s: (prev(s) // ne, prev(s) % ne)),
        out_shape=jax.ShapeDtypeStruct((t, u.shape[0]), BF16),
        scratch_shapes=[pltpu.VMEM((te, tt), F32), pltpu.VMEM((te, tt), F32)],
        compiler_params=_cparams(("arbitrary",), vmem),
        name="peer_gate",
    )(h, u, s1, e1, s2, e2, tau)


def _rope_tables(n_ctx, seq, dim):
    quarter = dim // 4
    pos = jnp.arange(seq, dtype=jnp.int32)
    rows, cols = (pos // GRID_W).astype(F32), (pos % GRID_W).astype(F32)
    inv = ROPE_THETA ** (-jnp.arange(quarter, dtype=F32) / quarter)
    parts_c, parts_s = [], []
    for p in (rows, cols):
        ang = p[:, None] * inv[None, :]
        c, s = jnp.cos(ang), jnp.sin(ang)
        parts_c += [c, c]
        parts_s += [-s, s]
    cos = jnp.concatenate(parts_c, axis=1)
    sin = jnp.concatenate(parts_s, axis=1)
    reps = LANES // dim
    cos, sin = jnp.tile(cos, (1, reps)), jnp.tile(sin, (1, reps))
    cos = jnp.concatenate([jnp.ones((n_ctx, LANES), F32), cos], axis=0)
    sin = jnp.concatenate([jnp.zeros((n_ctx, LANES), F32), sin], axis=0)
    return cos, sin


def kernel(x, c, ctx, c_ctx, ada_w, ada_b, ln_g, ln_b, na_w_qkv, na_rpb, na_w_o, mla_w_dq, mla_q_norm, mla_w_uq, mla_w_dkv, mla_kv_norm, mla_w_ukv, mla_w_o, gqa_w_q, gqa_w_k, gqa_w_v, gqa_q_norm, gqa_k_norm, gqa_w_o, peer_w_query, peer_sub_keys, peer_u, peer_v):
    batch, seq, d = x.shape
    n_ctx = ctx.shape[1]
    depth = ada_w.shape[0]
    assert batch == 1 and seq % GRID_W == 0 and n_ctx % MXU_DIM == 0
    t = n_ctx + seq
    n_rows = seq // GRID_W
    alpha = (2 * depth) ** 0.25

    tm_row = 128
    tm_mm = _largest_divisor(t, (768, 512, 256))
    tq, tk = 256, 512
    tt_route = 128
    tt_dense = _largest_divisor(t, (768, 256))
    assert seq % tk == 0 and n_ctx % tq == 0 and n_rows >= 2 * NA_WIN_H
    ncb = n_ctx // tm_row

    xs = jnp.concatenate([ctx[0], x[0]], axis=0)
    cmat = jnp.zeros((8, d), F32).at[0].set(c_ctx).at[1].set(c[0])
    mod = _adaln(cmat, ada_w, ada_b)[:, :2].reshape(depth, 2, N_MOD, 1, d)
    mods = lambda i, k: mod[i, :, k]

    cos128, sin128 = _rope_tables(n_ctx, seq, GQA_HEAD_DIM)
    cos64, sin64 = _rope_tables(n_ctx, seq, MLA_ROPE)

    h = _modulate(xs, mods(0, 1), mods(0, 0), n_ctx_blocks=ncb, tm=tm_row)
    for i in range(depth):
        kind, j = i % N_MIXERS, i // N_MIXERS
        if kind == 0:
            qkv = _mm(h, na_w_qkv[j].astype(BF16), out_dtype=BF16, tm=tm_mm)
            o = _natten(qkv, _natten_tables(na_rpb[j], n_rows), n_ctx=n_ctx, n_rows=n_rows)
            y = _mm(o, na_w_o[j].astype(BF16), out_dtype=F32, tm=tm_mm)
        elif kind == 1:
            qk_scale = (MLA_NOPE + MLA_ROPE) ** -0.5 * LOG2E
            rank = mla_w_uq.shape[1]
            w_uq = mla_w_uq[j].reshape(rank, MLA_HEADS, MLA_NOPE + MLA_ROPE)
            w_uq = jnp.pad(w_uq, ((0, 0), (0, 0), (0, MXU_DIM - MLA_NOPE - MLA_ROPE)))
            w_uq = w_uq.reshape(rank, MLA_HEADS * MXU_DIM).astype(BF16)
            w_pe = mla_w_dkv[j][:, MLA_KV_RANK:]
            w_dkv = jnp.concatenate([mla_w_dkv[j][:, :MLA_KV_RANK], w_pe, w_pe], axis=1).astype(BF16)
            cq = _mm(h, mla_w_dq[j].astype(BF16), out_dtype=F32, tm=tm_mm)
            q_raw = _mm(cq, w_uq, out_dtype=F32, tm=tm_mm, rms_gain=mla_q_norm[j])
            q = _rope_rows(q_raw, cos64, sin64, None, modes=("scale", "rope"), half=MLA_ROPE // 4,
                           scale=qk_scale, tm=tm_row)
            ckv = _mm(h, w_dkv, out_dtype=F32, tm=tm_mm)
            kv = _mm(ckv[:, :MLA_KV_RANK], mla_w_ukv[j].astype(BF16), out_dtype=BF16, tm=tm_mm,
                     rms_gain=mla_kv_norm[j])
            kpe = _rope_rows(ckv[:, MLA_KV_RANK:], cos64, sin64, None, modes=("rope",),
                             half=MLA_ROPE // 4, scale=1.0, tm=tm_row)
            hg = 4
            o = _flash(q, kv, None, kpe, n_steps=MLA_HEADS // hg, groups=hg, dq=MXU_DIM,
                       kv_width=hg * (MLA_NOPE + MLA_V),
                       k_off=tuple(g * (MLA_NOPE + MLA_V) for g in range(hg)),
                       v_off=tuple(g * (MLA_NOPE + MLA_V) + MLA_NOPE for g in range(hg)),
                       n_ctx=n_ctx, tq=tq, tk=tk)
            y = _mm(o, mla_w_o[j].astype(BF16), out_dtype=F32, tm=tm_mm)
        else:
            qk_scale = GQA_HEAD_DIM ** -0.5 * LOG2E
            q_raw = _mm(h, gqa_w_q[j].astype(BF16), out_dtype=F32, tm=tm_mm)
            k_raw = _mm(h, gqa_w_k[j].astype(BF16), out_dtype=F32, tm=tm_mm)
            vv = _mm(h, gqa_w_v[j].astype(BF16), out_dtype=BF16, tm=tm_mm)
            q = _rope_rows(q_raw, cos128, sin128, gqa_q_norm[j], modes=("normrope",),
                           half=GQA_HEAD_DIM // 4, scale=qk_scale, tm=tm_row)
            kk = _rope_rows(k_raw, cos128, sin128, gqa_k_norm[j], modes=("normrope",),
                            half=GQA_HEAD_DIM // 4, scale=1.0, tm=tm_row)
            rep = GQA_HEADS // GQA_KV_HEADS
            o = _flash(q, kk, vv, None, n_steps=GQA_KV_HEADS, groups=rep, dq=GQA_HEAD_DIM,
                       kv_width=GQA_HEAD_DIM, k_off=(0,) * rep, v_off=(0,) * rep,
                       n_ctx=n_ctx, tq=tq, tk=tk)
            y = _mm(o, gqa_w_o[j].astype(BF16), out_dtype=F32, tm=tm_mm)

        xs, h2 = _ln_mod(xs, y, mods(i, 2), ln_g[i, 0], ln_b[i, 0], mods(i, 4), mods(i, 3),
                         alpha=alpha, n_ctx_blocks=ncb, tm=tm_row)
        pq = _mm(h2, peer_w_query[i].astype(BF16), out_dtype=F32, tm=tm_mm)
        s1, s2, e1, e2, tau = _peer_route(pq, peer_sub_keys[i], tt=tt_route)
        w = _peer_gate(h2, peer_u[i].astype(BF16), s1, s2, e1, e2, tau, tt=tt_dense)
        f = _mm(w, peer_v[i].astype(BF16), out_dtype=F32, tm=_largest_divisor(t, (384, 256)), tn=512)
        if i + 1 < depth:
            xs, h = _ln_mod(xs, f, mods(i, 5), ln_g[i, 1], ln_b[i, 1], mods(i + 1, 1), mods(i + 1, 0),
                            alpha=alpha, n_ctx_blocks=ncb, tm=tm_row)
        else:
            xs, _ = _ln_mod(xs, f, mods(i, 5), ln_g[i, 1], ln_b[i, 1], None, None,
                            alpha=alpha, n_ctx_blocks=ncb, tm=tm_row)
    return xs[n_ctx:][None]
```

```python
import functools
import math

import numpy as np
import jax
import jax.numpy as jnp
from jax import lax
from jax.experimental import pallas as pl
from jax.experimental.pallas import tpu as pltpu

F32 = jnp.float32
BF16 = jnp.bfloat16

GRID_W = 64
ROPE_THETA = 10000.0
RMS_EPS = 1e-6
LN_EPS = 1e-5
N_MOD = 6
NA_HEADS = 32
NA_WIN_H = 8
NA_WIN_W = 16
MLA_HEADS = 32
MLA_NOPE = 128
MLA_ROPE = 64
MLA_V = 128
MLA_KV_RANK = 512
GQA_HEADS = 32
GQA_KV_HEADS = 8
GQA_HEAD_DIM = 128
PEER_HEADS = 8
PEER_N_KEYS = 96
PEER_TOPK = 16
PEER_HALF = 128
N_MIXERS = 3

LANES = 128
MXU_DIM = 256
VMEM_LIMIT_CAP = 58 * 1024 * 1024

NEG_BIG = -1e30
LOG2E = 1.4426950408889634

NT_DIMS = (((1,), (1,)), ((), ()))
TN_DIMS = (((0,), (0,)), ((), ()))


def _largest_divisor(n, candidates):
    return next(c for c in candidates if n % c == 0)


def _cparams(sem, vmem_bytes, flags=None):
    limit = int(min(VMEM_LIMIT_CAP, max(32 * 1024 * 1024, vmem_bytes + (8 << 20))))
    return pltpu.CompilerParams(dimension_semantics=sem, vmem_limit_bytes=limit, flags=flags)


def _adaln_kernel(c_ref, w_ref, b_ref, o_ref):
    c = c_ref[...]
    cond = c * jax.nn.sigmoid(c)
    o_ref[0] = jnp.dot(cond, w_ref[0], preferred_element_type=F32,
                       precision=lax.Precision.HIGHEST) + b_ref[0]


def _adaln(cmat, ada_w, ada_b):
    depth, d, n = ada_w.shape
    tn = 512
    return pl.pallas_call(
        _adaln_kernel,
        grid=(depth, n // tn),
        in_specs=[
            pl.BlockSpec((8, d), lambda i, j: (0, 0)),
            pl.BlockSpec((1, d, tn), lambda i, j: (i, 0, j)),
            pl.BlockSpec((1, 1, tn), lambda i, j: (i, 0, j)),
        ],
        out_specs=pl.BlockSpec((1, 8, tn), lambda i, j: (i, 0, j)),
        out_shape=jax.ShapeDtypeStruct((depth, 8, n), F32),
        compiler_params=_cparams(("parallel", "parallel"), 2 * d * tn * 4),
        name="adaln",
    )(cmat, ada_w, ada_b.reshape(depth, 1, n))


def _modulate_kernel(x_ref, sc_ref, sh_ref, h_ref):
    h_ref[...] = (x_ref[...] * (1.0 + sc_ref[0]) + sh_ref[0]).astype(h_ref.dtype)


def _modulate(x, sc, sh, *, n_ctx_blocks, tm):
    t, d = x.shape
    kind = lambda i: (jnp.where(i < n_ctx_blocks, 0, 1), 0, 0)
    return pl.pallas_call(
        _modulate_kernel,
        grid=(t // tm,),
        in_specs=[pl.BlockSpec((tm, d), lambda i: (i, 0)),
                  pl.BlockSpec((1, 1, d), kind), pl.BlockSpec((1, 1, d), kind)],
        out_specs=pl.BlockSpec((tm, d), lambda i: (i, 0)),
        out_shape=jax.ShapeDtypeStruct((t, d), BF16),
        compiler_params=_cparams(("parallel",), 2 * tm * d * 6),
        name="modulate",
    )(x, sc, sh)


def _ln_mod_kernel(*refs, alpha, with_h):
    if with_h:
        x_ref, y_ref, g_ref, lng_ref, lnb_ref, sc_ref, sh_ref, xo_ref, ho_ref = refs
    else:
        x_ref, y_ref, g_ref, lng_ref, lnb_ref, xo_ref = refs
    v = alpha * x_ref[...] + g_ref[0] * y_ref[...].astype(F32)
    mu = jnp.mean(v, axis=-1, keepdims=True)
    vc = v - mu
    var = jnp.mean(vc * vc, axis=-1, keepdims=True)
    xn = vc * lax.rsqrt(var + LN_EPS) * lng_ref[...] + lnb_ref[...]
    xo_ref[...] = xn
    if with_h:
        ho_ref[...] = (xn * (1.0 + sc_ref[0]) + sh_ref[0]).astype(ho_ref.dtype)


def _ln_mod(x, y, gate, lng, lnb, sc, sh, *, alpha, n_ctx_blocks, tm):
    t, d = x.shape
    with_h = sc is not None
    kind = lambda i: (jnp.where(i < n_ctx_blocks, 0, 1), 0, 0)
    row = pl.BlockSpec((tm, d), lambda i: (i, 0))
    vec = pl.BlockSpec((1, d), lambda i: (0, 0))
    in_specs = [row, row, pl.BlockSpec((1, 1, d), kind), vec, vec]
    args = [x, y, gate, lng.reshape(1, d), lnb.reshape(1, d)]
    out_shape = [jax.ShapeDtypeStruct((t, d), F32)]
    out_specs = [row]
    if with_h:
        in_specs += [pl.BlockSpec((1, 1, d), kind), pl.BlockSpec((1, 1, d), kind)]
        args += [sc, sh]
        out_shape.append(jax.ShapeDtypeStruct((t, d), BF16))
        out_specs.append(row)
    res = pl.pallas_call(
        functools.partial(_ln_mod_kernel, alpha=alpha, with_h=with_h),
        grid=(t // tm,),
        in_specs=in_specs, out_specs=out_specs, out_shape=out_shape,
        compiler_params=_cparams(("parallel",), 2 * tm * d * (4 + y.dtype.itemsize + 4 + 2)),
        name="ln_mod",
    )(*args)
    return (res[0], res[1]) if with_h else (res[0], None)


def _rope_rows_kernel(*refs, modes, half, scale, has_gain):
    if has_gain:
        x_ref, cos_ref, sin_ref, g_ref, o_ref = refs
    else:
        x_ref, cos_ref, sin_ref, o_ref = refs
    tm = x_ref.shape[0]
    cos = cos_ref[...]
    sin = sin_ref[...]
    lane = lax.broadcasted_iota(jnp.int32, (tm, LANES), 1)
    first = (lane % (2 * half)) < half
    for j in range(x_ref.shape[1] // LANES):
        mode = modes[j % len(modes)]
        x = x_ref[:, j * LANES:(j + 1) * LANES].astype(F32)
        if mode == "normrope":
            x = x * lax.rsqrt(jnp.mean(x * x, axis=-1, keepdims=True) + RMS_EPS) * g_ref[...]
        if mode in ("rope", "normrope"):
            partner = jnp.where(first, pltpu.roll(x, LANES - half, 1), pltpu.roll(x, half, 1))
            x = x * cos + partner * sin
        o_ref[:, j * LANES:(j + 1) * LANES] = (x * scale).astype(o_ref.dtype)


def _rope_rows(x, cos, sin, gain, *, modes, half, scale, tm):
    t, w = x.shape
    has_gain = gain is not None
    row = lambda width: pl.BlockSpec((tm, width), lambda i: (i, 0))
    in_specs = [row(w), row(LANES), row(LANES)]
    args = [x, cos, sin]
    if has_gain:
        in_specs.append(pl.BlockSpec((1, LANES), lambda i: (0, 0)))
        args.append(gain.reshape(1, LANES).astype(F32))
    return pl.pallas_call(
        functools.partial(_rope_rows_kernel, modes=modes, half=half, scale=scale, has_gain=has_gain),
        grid=(t // tm,),
        in_specs=in_specs, out_specs=row(w),
        out_shape=jax.ShapeDtypeStruct((t, w), BF16),
        compiler_params=_cparams(("parallel",), 2 * tm * w * (x.dtype.itemsize + 2)),
        name="rope_rows",
    )(*args)


def _mm_kernel(*refs, rms):
    if rms:
        a_ref, g_ref, b_ref, o_ref = refs
        x = a_ref[...].astype(F32)
        x = x * lax.rsqrt(jnp.mean(x * x, axis=-1, keepdims=True) + RMS_EPS) * g_ref[...]
        a = x.astype(BF16)
    else:
        a_ref, b_ref, o_ref = refs
        a = a_ref[...]
    o_ref[...] = jnp.dot(a, b_ref[...], preferred_element_type=F32).astype(o_ref.dtype)


def _mm(a, b, *, out_dtype, tm, tn=1024, rms_gain=None):
    m, k = a.shape
    _, n = b.shape
    tn = min(n, tn)
    rms = rms_gain is not None
    in_specs = [pl.BlockSpec((tm, k), lambda j, i: (i, 0))]
    args = [a]
    if rms:
        in_specs.append(pl.BlockSpec((1, k), lambda j, i: (0, 0)))
        args.append(rms_gain.reshape(1, k).astype(F32))
    in_specs.append(pl.BlockSpec((k, tn), lambda j, i: (0, j)))
    args.append(b)
    vmem = 2 * (tm * k * a.dtype.itemsize + k * tn * 2 + tm * tn * jnp.dtype(out_dtype).itemsize)
    return pl.pallas_call(
        functools.partial(_mm_kernel, rms=rms),
        grid=(n // tn, m // tm),
        in_specs=in_specs,
        out_specs=pl.BlockSpec((tm, tn), lambda j, i: (i, j)),
        out_shape=jax.ShapeDtypeStruct((m, n), out_dtype),
        compiler_params=_cparams(("parallel", "parallel"), vmem),
        name="mm",
    )(*args)


NA_QB = 2 * GRID_W
NA_SLAB_ROWS = NA_WIN_H + 1
NA_SLAB = NA_SLAB_ROWS * GRID_W


def _natten_tables(rpb, n_rows):
    heads = rpb.shape[0]
    deltas = (0, -2, -4, -5, -7)
    r0s = (0, 2, 4, n_rows - 4, n_rows - 2)
    shifts = np.zeros((2 * NA_WIN_W - 1, GRID_W, GRID_W), np.float32)
    colmask = np.full((GRID_W, GRID_W), NEG_BIG, np.float32)
    for qc in range(GRID_W):
        sc = min(max(qc - NA_WIN_W // 2, 0), GRID_W - NA_WIN_W)
        for kc in range(sc, sc + NA_WIN_W):
            shifts[kc - qc + NA_WIN_W - 1, qc, kc] = 1.0
            colmask[qc, kc] = 0.0
    toe = jnp.einsum("hrd,dqk->hrqk", rpb.astype(F32), shifts, precision=lax.Precision.HIGHEST) + colmask
    masked = jnp.full((heads, GRID_W, GRID_W), NEG_BIG, F32)
    cases = []
    for delta, r0 in zip(deltas, r0s):
        s0 = r0 + delta
        halves = []
        for qr in (r0, r0 + 1):
            sr = min(max(qr - NA_WIN_H // 2, 0), n_rows - NA_WIN_H)
            assert s0 <= sr and sr + NA_WIN_H <= s0 + NA_SLAB_ROWS
            halves.append(jnp.concatenate(
                [toe[:, kr - qr + NA_WIN_H - 1] if sr <= kr < sr + NA_WIN_H else masked
                 for kr in range(s0, s0 + NA_SLAB_ROWS)], axis=-1))
        cases.append(jnp.concatenate(halves, axis=1))
    cases.append(jnp.full((heads, NA_QB, NA_SLAB), NEG_BIG, F32))
    return jnp.stack(cases, axis=1)


def _natten_kernel(q_ref, k_ref, v_ref, tab_ref, o_ref, *, n_ctx, n_rows, scale):
    n_ctx_blocks = n_ctx // NA_QB
    n_blocks = q_ref.shape[0] // NA_QB
    kc = k_ref[0:n_ctx, :]
    vc = v_ref[0:n_ctx, :]

    def body(b, carry):
        r0 = 2 * (b - n_ctx_blocks)
        s0 = jnp.minimum(jnp.clip(r0 - NA_WIN_H // 2, 0, n_rows - NA_WIN_H), n_rows - NA_SLAB_ROWS)
        delta = s0 - r0
        case = jnp.where(b < n_ctx_blocks, 5,
                         jnp.where(delta == 0, 0, jnp.where(delta == -2, 1, jnp.where(
                             delta == -4, 2, jnp.where(delta == -5, 3, 4)))))
        row0 = pl.multiple_of(n_ctx + jnp.where(b < n_ctx_blocks, 0, s0) * GRID_W, GRID_W)
        q0 = pl.multiple_of(b * NA_QB, NA_QB)
        q = q_ref[pl.ds(q0, NA_QB), :]
        ks = k_ref[pl.ds(row0, NA_SLAB), :]
        vs = v_ref[pl.ds(row0, NA_SLAB), :]
        s_loc = lax.dot_general(q, ks, NT_DIMS, preferred_element_type=F32) * scale + tab_ref[0, case]
        s_ctx = lax.dot_general(q, kc, NT_DIMS, preferred_element_type=F32) * scale
        m = jnp.maximum(jnp.max(s_loc, axis=-1, keepdims=True), jnp.max(s_ctx, axis=-1, keepdims=True))
        p_loc = jnp.exp(s_loc - m)
        p_ctx = jnp.exp(s_ctx - m)
        l = jnp.sum(p_loc, axis=-1, keepdims=True) + jnp.sum(p_ctx, axis=-1, keepdims=True)
        o = (jnp.dot(p_loc.astype(BF16), vs, preferred_element_type=F32)
             + jnp.dot(p_ctx.astype(BF16), vc, preferred_element_type=F32))
        o_ref[pl.ds(q0, NA_QB), :] = (o / l).astype(o_ref.dtype)
        return carry

    lax.fori_loop(0, n_blocks, body, 0, unroll=2)


def _natten(qkv, tables, *, n_ctx, n_rows):
    t = qkv.shape[0]
    heads, dh = NA_HEADS, LANES
    col = lambda off: pl.BlockSpec((t, dh), lambda h: (0, off + h))
    return pl.pallas_call(
        functools.partial(_natten_kernel, n_ctx=n_ctx, n_rows=n_rows, scale=dh ** -0.5),
        grid=(heads,),
        in_specs=[col(0), col(heads), col(2 * heads),
                  pl.BlockSpec((1, 6, NA_QB, NA_SLAB), lambda h: (h, 0, 0, 0))],
        out_specs=pl.BlockSpec((t, dh), lambda h: (0, h)),
        out_shape=jax.ShapeDtypeStruct((t, heads * dh), BF16),
        compiler_params=_cparams(("parallel",), 2 * (4 * t * dh * 2 + 6 * NA_QB * NA_SLAB * 4)),
        name="natten",
    )(qkv, qkv, qkv, tables)


def _flash_kernel(*refs, groups, dq, dv, n_ctx, tk, tiles_per_trip, k_off, v_off, has_pe, shared_kv):
    refs = list(refs)
    q_ref = refs.pop(0)
    k_ref = refs.pop(0)
    kpe_ref = refs.pop(0) if has_pe else None
    v_ref = k_ref if shared_kv else refs.pop(0)
    o_ref, acc_ref, sa_ref, sb_ref = refs
    tq = q_ref.shape[0]
    t = k_ref.shape[0]
    qi = pl.program_id(1)
    n_main = jnp.where(qi * tq < n_ctx, 0, (t - n_ctx) // tk)

    def scores(g, start, size):
        kt = k_ref[pl.ds(start, size), k_off[g]:k_off[g] + LANES]
        if has_pe:
            kt = jnp.concatenate([kt, kpe_ref[pl.ds(start, size), :]], axis=1)
        return lax.dot_general(kt, q_ref[:, g * dq:(g + 1) * dq], NT_DIMS, preferred_element_type=F32)

    def pv(g, start, size, p):
        vt = v_ref[pl.ds(start, size), v_off[g]:v_off[g] + dv]
        return lax.dot_general(vt, p.astype(BF16), TN_DIMS, preferred_element_type=F32)

    carry = []
    for g in range(groups):
        s_t = scores(g, 0, n_ctx)
        m0 = jnp.max(s_t, axis=0, keepdims=True)
        p = jnp.exp2(s_t - m0)
        carry += [m0, jnp.sum(p, axis=0, keepdims=True)]
        acc_ref[g] = pv(g, 0, n_ctx, p)
        sa_ref[g] = scores(g, n_ctx, tk)

    def process(s_ref, start, carry):
        out = []
        for g in range(groups):
            m, l = carry[2 * g], carry[2 * g + 1]
            s_t = s_ref[g]
            m_new = jnp.maximum(m, jnp.max(s_t, axis=0, keepdims=True))
            alpha = jnp.exp2(m - m_new)
            p = jnp.exp2(s_t - m_new)
            out += [m_new, alpha * l + jnp.sum(p, axis=0, keepdims=True)]
            acc_ref[g] = alpha * acc_ref[g] + pv(g, start, tk, p)
        return tuple(out)

    def body(j, carry):
        base = n_ctx + tiles_per_trip * j * tk
        bufs = (sa_ref, sb_ref)
        for i in range(tiles_per_trip):
            cur = pl.multiple_of(base + i * tk, MXU_DIM)
            nxt = pl.multiple_of(jnp.minimum(base + (i + 1) * tk, t - tk), MXU_DIM)
            for g in range(groups):
                bufs[(i + 1) % 2][g] = scores(g, nxt, tk)
            carry = process(bufs[i % 2], cur, carry)
        return carry

    carry = lax.fori_loop(0, n_main // tiles_per_trip, body, tuple(carry))
    for g in range(groups):
        o_t = acc_ref[g] * (1.0 / carry[2 * g + 1])
        o_ref[:, g * dv:(g + 1) * dv] = o_t.T.astype(o_ref.dtype)


def _flash(q, k, v, kpe, *, n_steps, groups, dq, kv_width, k_off, v_off, n_ctx, tq, tk):
    t = q.shape[0]
    dv = LANES
    has_pe = kpe is not None
    shared_kv = v is None
    once = pl.Buffered(1)
    kv_spec = pl.BlockSpec((t, kv_width), lambda h, i: (0, h), pipeline_mode=once)
    in_specs = [pl.BlockSpec((tq, groups * dq), lambda h, i: (i, h)), kv_spec]
    args = [q, k]
    if has_pe:
        in_specs.append(pl.BlockSpec((t, LANES), lambda h, i: (0, 0), pipeline_mode=once))
        args.append(kpe)
    if not shared_kv:
        in_specs.append(kv_spec)
        args.append(v)
    tiles_per_trip = _largest_divisor((t - n_ctx) // tk, (8, 4, 2))
    vmem = (2 * tq * groups * (dq + dv) * 2 + (1 if shared_kv else 2) * t * kv_width * 2
            + t * LANES * 2 + groups * (dv * tq * 4 + 4 * tk * tq * 4))
    return pl.pallas_call(
        functools.partial(_flash_kernel, groups=groups, dq=dq, dv=dv, n_ctx=n_ctx, tk=tk,
                          tiles_per_trip=tiles_per_trip, k_off=k_off,
                          v_off=v_off, has_pe=has_pe, shared_kv=shared_kv),
        grid=(n_steps, t // tq),
        in_specs=in_specs,
        out_specs=pl.BlockSpec((tq, groups * dv), lambda h, i: (i, h)),
        out_shape=jax.ShapeDtypeStruct((t, n_steps * groups * dv), BF16),
        scratch_shapes=[pltpu.VMEM((groups, dv, tq), F32), pltpu.VMEM((groups, tk, tq), F32),
                        pltpu.VMEM((groups, tk, tq), F32)],
        compiler_params=_cparams(("parallel", "parallel"), vmem),
        name="flash",
    )(*args)


def _take_top(s, rowid, count):
    sentinel = s.shape[0]
    cur = s
    out = []
    for _ in range(count):
        m = jnp.max(cur, axis=0, keepdims=True)
        first = jnp.min(jnp.where(cur == m, rowid, sentinel), axis=0, keepdims=True)
        cur = jnp.where(rowid == first, -jnp.inf, cur)
        out.append(m)
    return out


_PEER_PAIRS = tuple((i, j) for i in range(PEER_TOPK) for j in range(PEER_TOPK) if (i + 1) * (j + 1) <= PEER_TOPK)


def _route_kernel(q_ref, sk_ref, a1_ref, b2_ref, tau_ref, cand_scr):
    tt = q_ref.shape[0]
    nk = sk_ref.shape[2]
    rowid = lax.broadcasted_iota(jnp.int32, (nk, tt), 0)
    n_pairs = len(_PEER_PAIRS)
    n_pad = -n_pairs % 8
    rowid_c = lax.broadcasted_iota(jnp.int32, (n_pairs + n_pad, tt), 0)

    def pair_sums(x1, x2, pad_value):
        for r, (i, j) in enumerate(_PEER_PAIRS):
            cand_scr[r:r + 1, :] = x1[i] + x2[j]
        for r in range(n_pairs, n_pairs + n_pad):
            cand_scr[r:r + 1, :] = jnp.full((1, tt), pad_value, F32)
        return cand_scr[...]

    def body(p, carry):
        def scores(hf):
            c0 = pl.multiple_of((2 * p + hf) * PEER_HALF, PEER_HALF)
            qs = q_ref[:, pl.ds(c0, PEER_HALF)].astype(BF16)
            keys = sk_ref[p, hf].astype(BF16)
            return lax.dot_general(keys, qs, NT_DIMS, preferred_element_type=F32)

        s1 = scores(0)
        s2 = scores(1)
        t1 = _take_top(s1, rowid, PEER_TOPK + 1)
        t2 = _take_top(s2, rowid, PEER_TOPK + 1)
        b1 = (s1 - t1[0]) * LOG2E
        b2 = (s2 - t2[0]) * LOG2E
        bt1 = [(x - t1[0]) * LOG2E for x in t1]
        bt2 = [(x - t2[0]) * LOG2E for x in t2]
        cand = pair_sums(bt1, bt2, -jnp.inf)
        selected = cand >= _take_top(cand, rowid_c, PEER_TOPK)[-1]
        log_z = jnp.log2(jnp.sum(jnp.where(selected, jnp.exp2(cand), 0.0), axis=0, keepdims=True))
        at1 = [x - log_z for x in bt1]
        shifted = pair_sums(at1, bt2, -jnp.inf)
        last_in = jnp.min(jnp.where(selected, shifted, jnp.inf), axis=0, keepdims=True)
        first_out = jnp.max(jnp.where(selected, -jnp.inf, shifted), axis=0, keepdims=True)
        first_out = jnp.maximum(first_out, jnp.maximum(at1[PEER_TOPK] + bt2[0], at1[0] + bt2[PEER_TOPK]))
        a1_ref[p] = b1 - log_z
        b2_ref[p] = b2
        tau_ref[p] = 0.5 * (last_in + first_out)
        return carry

    lax.fori_loop(0, PEER_HEADS, body, 0, unroll=2)


def _peer_route(q, sub_keys, *, tt):
    t = q.shape[0]
    heads, _, nk, half = sub_keys.shape
    big = pl.BlockSpec((heads, nk, tt), lambda i: (0, 0, i))
    big_shape = jax.ShapeDtypeStruct((heads, nk, t), F32)
    return pl.pallas_call(
        _route_kernel,
        grid=(t // tt,),
        in_specs=[pl.BlockSpec((tt, q.shape[1]), lambda i: (i, 0)),
                  pl.BlockSpec(sub_keys.shape, lambda i: (0, 0, 0, 0))],
        out_specs=[big, big, pl.BlockSpec((heads, 1, tt), lambda i: (0, 0, i))],
        out_shape=[big_shape, big_shape, jax.ShapeDtypeStruct((heads, 1, t), F32)],
        scratch_shapes=[pltpu.VMEM((len(_PEER_PAIRS) + (-len(_PEER_PAIRS) % 8), tt), F32)],
        compiler_params=_cparams(("parallel",), 2 * (tt * q.shape[1] * 4 + 2 * heads * nk * tt * 4)),
        name="peer_route",
    )(q, sub_keys)


PEER_K1_TILE = 8


def _peer_gate_kernel(h_ref, u_ref, a1_ref, b2_ref, tau_ref, o_ref, a_even, a_odd):
    s = pl.program_id(0)
    tt = h_ref.shape[0]
    heads, nk, _ = b2_ref.shape

    @pl.when(s == 0)
    def _():
        a_odd[...] = jnp.zeros_like(a_odd)

    def gated(a_read, a, ls):
        gates = jnp.zeros((nk, LANES), F32)
        for p in range(heads):
            c = a1_ref[p, a, :, ls] + b2_ref[p, :, ls]
            gates = gates + jnp.where(c >= tau_ref[p, :, ls], jnp.exp2(c), 0.0)
        act = a_read[a * nk:(a + 1) * nk, ls]
        gelu = 0.5 * act * (1.0 + lax.erf(act * (2.0 ** -0.5)))
        return gelu * gates

    def step(a_write, a_read):
        def chunk(n, carry):
            n0 = pl.multiple_of(n * MXU_DIM, MXU_DIM)
            a_write[:, pl.ds(n0, MXU_DIM)] = lax.dot_general(
                u_ref[...], h_ref[pl.ds(n0, MXU_DIM), :], NT_DIMS, preferred_element_type=F32)
            for c in range(MXU_DIM // LANES):
                ls = pl.ds(pl.multiple_of(n0 + c * LANES, LANES), LANES)
                cols = [gated(a_read, a, ls) for a in range(PEER_K1_TILE)]
                o_ref[ls, :] = jnp.concatenate(cols, axis=0).T.astype(o_ref.dtype)
            return carry

        lax.fori_loop(0, tt // MXU_DIM, chunk, 0)

    @pl.when(s % 2 == 0)
    def _():
        step(a_even, a_odd)

    @pl.when(s % 2 == 1)
    def _():
        step(a_odd, a_even)


def _peer_gate(h, u, a1, b2, tau, *, tt):
    t, d = h.shape
    heads, nk, _ = a1.shape
    te = PEER_K1_TILE * nk
    ne = u.shape[0] // te
    n_pairs = (t // tt) * ne
    cur = lambda s: jnp.minimum(s, n_pairs - 1)
    prev = lambda s: jnp.maximum(s - 1, 0)
    tok = pl.BlockSpec((heads, nk, tt), lambda s: (0, 0, prev(s) // ne))
    k1 = pl.BlockSpec((heads, PEER_K1_TILE, 1, tt), lambda s: (0, prev(s) % ne, 0, prev(s) // ne))
    a1 = a1.reshape(heads, nk, 1, t)
    vmem = (2 * tt * d * 2 + 2 * te * d * 2 + 2 * te * tt * 4 + 2 * tt * te * 2
            + 2 * heads * nk * tt * 4 + 2 * heads * PEER_K1_TILE * 8 * tt * 4)
    return pl.pallas_call(
        _peer_gate_kernel,
        grid=(n_pairs + 1,),
        in_specs=[pl.BlockSpec((tt, d), lambda s: (cur(s) // ne, 0)),
                  pl.BlockSpec((te, d), lambda s: (cur(s) % ne, 0)),
                  k1, tok,
                  pl.BlockSpec((heads, 1, tt), lambda s: (0, 0, prev(s) // ne))],
        out_specs=pl.BlockSpec((tt, te), lambda s: (prev(s) // ne, prev(s) % ne)),
        out_shape=jax.ShapeDtypeStruct((t, u.shape[0]), BF16),
        scratch_shapes=[pltpu.VMEM((te, tt), F32), pltpu.VMEM((te, tt), F32)],
        compiler_params=_cparams(("arbitrary",), vmem),
        name="peer_gate",
    )(h, u, a1, b2, tau)


def _rope_tables(n_ctx, seq, dim):
    quarter = dim // 4
    pos = jnp.arange(seq, dtype=jnp.int32)
    rows, cols = (pos // GRID_W).astype(F32), (pos % GRID_W).astype(F32)
    inv = ROPE_THETA ** (-jnp.arange(quarter, dtype=F32) / quarter)
    parts_c, parts_s = [], []
    for p in (rows, cols):
        ang = p[:, None] * inv[None, :]
        c, s = jnp.cos(ang), jnp.sin(ang)
        parts_c += [c, c]
        parts_s += [-s, s]
    cos = jnp.concatenate(parts_c, axis=1)
    sin = jnp.concatenate(parts_s, axis=1)
    reps = LANES // dim
    cos, sin = jnp.tile(cos, (1, reps)), jnp.tile(sin, (1, reps))
    cos = jnp.concatenate([jnp.ones((n_ctx, LANES), F32), cos], axis=0)
    sin = jnp.concatenate([jnp.zeros((n_ctx, LANES), F32), sin], axis=0)
    return cos, sin


def kernel(x, c, ctx, c_ctx, ada_w, ada_b, ln_g, ln_b, na_w_qkv, na_rpb, na_w_o, mla_w_dq, mla_q_norm, mla_w_uq, mla_w_dkv, mla_kv_norm, mla_w_ukv, mla_w_o, gqa_w_q, gqa_w_k, gqa_w_v, gqa_q_norm, gqa_k_norm, gqa_w_o, peer_w_query, peer_sub_keys, peer_u, peer_v):
    batch, seq, d = x.shape
    n_ctx = ctx.shape[1]
    depth = ada_w.shape[0]
    assert batch == 1 and seq % GRID_W == 0 and n_ctx % MXU_DIM == 0
    t = n_ctx + seq
    n_rows = seq // GRID_W
    alpha = (2 * depth) ** 0.25

    tm_row = 128
    tm_mm = _largest_divisor(t, (768, 512, 256))
    tq, tk = 256, 512
    tt_route = 256
    tt_dense = _largest_divisor(t, (768, 256))
    assert seq % tk == 0 and n_ctx % tq == 0 and n_rows >= 2 * NA_WIN_H
    ncb = n_ctx // tm_row

    xs = jnp.concatenate([ctx[0], x[0]], axis=0)
    cmat = jnp.zeros((8, d), F32).at[0].set(c_ctx).at[1].set(c[0])
    mod = _adaln(cmat, ada_w, ada_b)[:, :2].reshape(depth, 2, N_MOD, 1, d)
    mods = lambda i, k: mod[i, :, k]

    cos128, sin128 = _rope_tables(n_ctx, seq, GQA_HEAD_DIM)
    cos64, sin64 = _rope_tables(n_ctx, seq, MLA_ROPE)

    h = _modulate(xs, mods(0, 1), mods(0, 0), n_ctx_blocks=ncb, tm=tm_row)
    for i in range(depth):
        kind, j = i % N_MIXERS, i // N_MIXERS
        if kind == 0:
            qkv = _mm(h, na_w_qkv[j].astype(BF16), out_dtype=BF16, tm=tm_mm)
            o = _natten(qkv, _natten_tables(na_rpb[j], n_rows), n_ctx=n_ctx, n_rows=n_rows)
            y = _mm(o, na_w_o[j].astype(BF16), out_dtype=BF16, tm=tm_mm)
        elif kind == 1:
            qk_scale = (MLA_NOPE + MLA_ROPE) ** -0.5 * LOG2E
            rank = mla_w_uq.shape[1]
            w_uq = mla_w_uq[j].reshape(rank, MLA_HEADS, MLA_NOPE + MLA_ROPE)
            w_uq = jnp.pad(w_uq, ((0, 0), (0, 0), (0, MXU_DIM - MLA_NOPE - MLA_ROPE)))
            w_uq = w_uq.reshape(rank, MLA_HEADS * MXU_DIM).astype(BF16)
            w_pe = mla_w_dkv[j][:, MLA_KV_RANK:]
            w_dkv = jnp.concatenate([mla_w_dkv[j][:, :MLA_KV_RANK], w_pe, w_pe], axis=1).astype(BF16)
            cq = _mm(h, mla_w_dq[j].astype(BF16), out_dtype=F32, tm=tm_mm)
            q_raw = _mm(cq, w_uq, out_dtype=F32, tm=tm_mm, rms_gain=mla_q_norm[j])
            q = _rope_rows(q_raw, cos64, sin64, None, modes=("scale", "rope"), half=MLA_ROPE // 4,
                           scale=qk_scale, tm=tm_row)
            ckv = _mm(h, w_dkv, out_dtype=F32, tm=tm_mm)
            kv = _mm(ckv[:, :MLA_KV_RANK], mla_w_ukv[j].astype(BF16), out_dtype=BF16, tm=tm_mm,
                     rms_gain=mla_kv_norm[j])
            kpe = _rope_rows(ckv[:, MLA_KV_RANK:], cos64, sin64, None, modes=("rope",),
                             half=MLA_ROPE // 4, scale=1.0, tm=tm_row)
            hg = 4
            o = _flash(q, kv, None, kpe, n_steps=MLA_HEADS // hg, groups=hg, dq=MXU_DIM,
                       kv_width=hg * (MLA_NOPE + MLA_V),
                       k_off=tuple(g * (MLA_NOPE + MLA_V) for g in range(hg)),
                       v_off=tuple(g * (MLA_NOPE + MLA_V) + MLA_NOPE for g in range(hg)),
                       n_ctx=n_ctx, tq=tq, tk=tk)
            y = _mm(o, mla_w_o[j].astype(BF16), out_dtype=BF16, tm=tm_mm)
        else:
            qk_scale = GQA_HEAD_DIM ** -0.5 * LOG2E
            q_raw = _mm(h, gqa_w_q[j].astype(BF16), out_dtype=F32, tm=tm_mm)
            k_raw = _mm(h, gqa_w_k[j].astype(BF16), out_dtype=F32, tm=tm_mm)
            vv = _mm(h, gqa_w_v[j].astype(BF16), out_dtype=BF16, tm=tm_mm)
            q = _rope_rows(q_raw, cos128, sin128, gqa_q_norm[j], modes=("normrope",),
                           half=GQA_HEAD_DIM // 4, scale=qk_scale, tm=tm_row)
            kk = _rope_rows(k_raw, cos128, sin128, gqa_k_norm[j], modes=("normrope",),
                            half=GQA_HEAD_DIM // 4, scale=1.0, tm=tm_row)
            rep = GQA_HEADS // GQA_KV_HEADS
            o = _flash(q, kk, vv, None, n_steps=GQA_KV_HEADS, groups=rep, dq=GQA_HEAD_DIM,
                       kv_width=GQA_HEAD_DIM, k_off=(0,) * rep, v_off=(0,) * rep,
                       n_ctx=n_ctx, tq=tq, tk=tk)
            y = _mm(o, gqa_w_o[j].astype(BF16), out_dtype=BF16, tm=tm_mm)

        xs, h2 = _ln_mod(xs, y, mods(i, 2), ln_g[i, 0], ln_b[i, 0], mods(i, 4), mods(i, 3),
                         alpha=alpha, n_ctx_blocks=ncb, tm=tm_row)
        pq = _mm(h2, peer_w_query[i].astype(BF16), out_dtype=F32, tm=tm_mm)
        a1, b2, tau = _peer_route(pq, peer_sub_keys[i], tt=tt_route)
        w = _peer_gate(h2, peer_u[i].astype(BF16), a1, b2, tau, tt=tt_dense)
        f = _mm(w, peer_v[i].astype(BF16), out_dtype=BF16, tm=_largest_divisor(t, (384, 256)), tn=512)
        if i + 1 < depth:
            xs, h = _ln_mod(xs, f, mods(i, 5), ln_g[i, 1], ln_b[i, 1], mods(i + 1, 1), mods(i + 1, 0),
                            alpha=alpha, n_ctx_blocks=ncb, tm=tm_row)
        else:
            xs, _ = _ln_mod(xs, f, mods(i, 5), ln_g[i, 1], ln_b[i, 1], None, None,
                            alpha=alpha, n_ctx_blocks=ncb, tm=tm_row)
    return xs[n_ctx:][None]
```

```python
import functools
import math

import numpy as np
import jax
import jax.numpy as jnp
from jax import lax
from jax.experimental import pallas as pl
from jax.experimental.pallas import tpu as pltpu

F32 = jnp.float32
BF16 = jnp.bfloat16

GRID_W = 64
ROPE_THETA = 10000.0
RMS_EPS = 1e-6
LN_EPS = 1e-5
N_MOD = 6
NA_HEADS = 32
NA_WIN_H = 8
NA_WIN_W = 16
MLA_HEADS = 32
MLA_NOPE = 128
MLA_ROPE = 64
MLA_V = 128
MLA_KV_RANK = 512
GQA_HEADS = 32
GQA_KV_HEADS = 8
GQA_HEAD_DIM = 128
PEER_HEADS = 8
PEER_N_KEYS = 96
PEER_TOPK = 16
PEER_HALF = 128
N_MIXERS = 3

LANES = 128
MXU_DIM = 256
VMEM_LIMIT_CAP = 58 * 1024 * 1024

NEG_BIG = -1e30
LOG2E = 1.4426950408889634

NT_DIMS = (((1,), (1,)), ((), ()))
TN_DIMS = (((0,), (0,)), ((), ()))


def _largest_divisor(n, candidates):
    return next(c for c in candidates if n % c == 0)


def _cparams(sem, vmem_bytes, flags=None):
    limit = int(min(VMEM_LIMIT_CAP, max(32 * 1024 * 1024, vmem_bytes + (8 << 20))))
    return pltpu.CompilerParams(dimension_semantics=sem, vmem_limit_bytes=limit, flags=flags)


def _adaln_kernel(c_ref, w_ref, b_ref, o_ref):
    c = c_ref[...]
    cond = c * jax.nn.sigmoid(c)
    o_ref[0] = jnp.dot(cond.astype(BF16), w_ref[0].astype(BF16), preferred_element_type=F32) + b_ref[0]


def _adaln(cmat, ada_w, ada_b):
    depth, d, n = ada_w.shape
    tn = 512
    return pl.pallas_call(
        _adaln_kernel,
        grid=(depth, n // tn),
        in_specs=[
            pl.BlockSpec((8, d), lambda i, j: (0, 0)),
            pl.BlockSpec((1, d, tn), lambda i, j: (i, 0, j)),
            pl.BlockSpec((1, 1, tn), lambda i, j: (i, 0, j)),
        ],
        out_specs=pl.BlockSpec((1, 8, tn), lambda i, j: (i, 0, j)),
        out_shape=jax.ShapeDtypeStruct((depth, 8, n), F32),
        compiler_params=_cparams(("parallel", "parallel"), 2 * d * tn * 4),
        name="adaln",
    )(cmat, ada_w, ada_b.reshape(depth, 1, n))


def _modulate_kernel(x_ref, sc_ref, sh_ref, h_ref):
    h_ref[...] = (x_ref[...] * (1.0 + sc_ref[0]) + sh_ref[0]).astype(h_ref.dtype)


def _modulate(x, sc, sh, *, n_ctx_blocks, tm):
    t, d = x.shape
    kind = lambda i: (jnp.where(i < n_ctx_blocks, 0, 1), 0, 0)
    return pl.pallas_call(
        _modulate_kernel,
        grid=(t // tm,),
        in_specs=[pl.BlockSpec((tm, d), lambda i: (i, 0)),
                  pl.BlockSpec((1, 1, d), kind), pl.BlockSpec((1, 1, d), kind)],
        out_specs=pl.BlockSpec((tm, d), lambda i: (i, 0)),
        out_shape=jax.ShapeDtypeStruct((t, d), BF16),
        compiler_params=_cparams(("parallel",), 2 * tm * d * 6),
        name="modulate",
    )(x, sc, sh)


def _ln_mod_kernel(*refs, alpha, with_h):
    if with_h:
        x_ref, y_ref, g_ref, lng_ref, lnb_ref, sc_ref, sh_ref, xo_ref, ho_ref = refs
    else:
        x_ref, y_ref, g_ref, lng_ref, lnb_ref, xo_ref = refs
    v = alpha * x_ref[...] + g_ref[0] * y_ref[...].astype(F32)
    mu = jnp.mean(v, axis=-1, keepdims=True)
    vc = v - mu
    var = jnp.mean(vc * vc, axis=-1, keepdims=True)
    xn = vc * lax.rsqrt(var + LN_EPS) * lng_ref[...] + lnb_ref[...]
    xo_ref[...] = xn
    if with_h:
        ho_ref[...] = (xn * (1.0 + sc_ref[0]) + sh_ref[0]).astype(ho_ref.dtype)


def _ln_mod(x, y, gate, lng, lnb, sc, sh, *, alpha, n_ctx_blocks, tm):
    t, d = x.shape
    with_h = sc is not None
    kind = lambda i: (jnp.where(i < n_ctx_blocks, 0, 1), 0, 0)
    row = pl.BlockSpec((tm, d), lambda i: (i, 0))
    vec = pl.BlockSpec((1, d), lambda i: (0, 0))
    in_specs = [row, row, pl.BlockSpec((1, 1, d), kind), vec, vec]
    args = [x, y, gate, lng.reshape(1, d), lnb.reshape(1, d)]
    out_shape = [jax.ShapeDtypeStruct((t, d), F32)]
    out_specs = [row]
    if with_h:
        in_specs += [pl.BlockSpec((1, 1, d), kind), pl.BlockSpec((1, 1, d), kind)]
        args += [sc, sh]
        out_shape.append(jax.ShapeDtypeStruct((t, d), BF16))
        out_specs.append(row)
    res = pl.pallas_call(
        functools.partial(_ln_mod_kernel, alpha=alpha, with_h=with_h),
        grid=(t // tm,),
        in_specs=in_specs, out_specs=out_specs, out_shape=out_shape,
        compiler_params=_cparams(("parallel",), 2 * tm * d * (4 + y.dtype.itemsize + 4 + 2)),
        name="ln_mod",
    )(*args)
    return (res[0], res[1]) if with_h else (res[0], None)


def _rope_rows_kernel(*refs, modes, half, scale, has_gain):
    if has_gain:
        x_ref, cos_ref, sin_ref, g_ref, o_ref = refs
    else:
        x_ref, cos_ref, sin_ref, o_ref = refs
    tm = x_ref.shape[0]
    cos = cos_ref[...]
    sin = sin_ref[...]
    lane = lax.broadcasted_iota(jnp.int32, (tm, LANES), 1)
    first = (lane % (2 * half)) < half
    for j in range(x_ref.shape[1] // LANES):
        mode = modes[j % len(modes)]
        x = x_ref[:, j * LANES:(j + 1) * LANES].astype(F32)
        if mode == "normrope":
            x = x * lax.rsqrt(jnp.mean(x * x, axis=-1, keepdims=True) + RMS_EPS) * g_ref[...]
        if mode in ("rope", "normrope"):
            partner = jnp.where(first, pltpu.roll(x, LANES - half, 1), pltpu.roll(x, half, 1))
            x = x * cos + partner * sin
        o_ref[:, j * LANES:(j + 1) * LANES] = (x * scale).astype(o_ref.dtype)


def _rope_rows(x, cos, sin, gain, *, modes, half, scale, tm):
    t, w = x.shape
    has_gain = gain is not None
    row = lambda width: pl.BlockSpec((tm, width), lambda i: (i, 0))
    in_specs = [row(w), row(LANES), row(LANES)]
    args = [x, cos, sin]
    if has_gain:
        in_specs.append(pl.BlockSpec((1, LANES), lambda i: (0, 0)))
        args.append(gain.reshape(1, LANES).astype(F32))
    return pl.pallas_call(
        functools.partial(_rope_rows_kernel, modes=modes, half=half, scale=scale, has_gain=has_gain),
        grid=(t // tm,),
        in_specs=in_specs, out_specs=row(w),
        out_shape=jax.ShapeDtypeStruct((t, w), BF16),
        compiler_params=_cparams(("parallel",), 2 * tm * w * (x.dtype.itemsize + 2)),
        name="rope_rows",
    )(*args)


def _mm_kernel(*refs, rms):
    if rms:
        a_ref, g_ref, b_ref, o_ref = refs
        x = a_ref[...].astype(F32)
        x = x * lax.rsqrt(jnp.mean(x * x, axis=-1, keepdims=True) + RMS_EPS) * g_ref[...]
        a = x.astype(BF16)
    else:
        a_ref, b_ref, o_ref = refs
        a = a_ref[...]
    o_ref[...] = jnp.dot(a, b_ref[...], preferred_element_type=F32).astype(o_ref.dtype)


def _mm(a, b, *, out_dtype, tm, tn=1024, rms_gain=None):
    m, k = a.shape
    _, n = b.shape
    tn = min(n, tn)
    rms = rms_gain is not None
    in_specs = [pl.BlockSpec((tm, k), lambda j, i: (i, 0))]
    args = [a]
    if rms:
        in_specs.append(pl.BlockSpec((1, k), lambda j, i: (0, 0)))
        args.append(rms_gain.reshape(1, k).astype(F32))
    in_specs.append(pl.BlockSpec((k, tn), lambda j, i: (0, j)))
    args.append(b)
    vmem = 2 * (tm * k * a.dtype.itemsize + k * tn * 2 + tm * tn * jnp.dtype(out_dtype).itemsize)
    return pl.pallas_call(
        functools.partial(_mm_kernel, rms=rms),
        grid=(n // tn, m // tm),
        in_specs=in_specs,
        out_specs=pl.BlockSpec((tm, tn), lambda j, i: (i, j)),
        out_shape=jax.ShapeDtypeStruct((m, n), out_dtype),
        compiler_params=_cparams(("parallel", "parallel"), vmem),
        name="mm",
    )(*args)


NA_QB = 2 * GRID_W
NA_SLAB_ROWS = NA_WIN_H + 1
NA_SLAB = NA_SLAB_ROWS * GRID_W


def _natten_tables(rpb, n_rows):
    heads = rpb.shape[0]
    deltas = (0, -2, -4, -5, -7)
    r0s = (0, 2, 4, n_rows - 4, n_rows - 2)
    shifts = np.zeros((2 * NA_WIN_W - 1, GRID_W, GRID_W), np.float32)
    colmask = np.full((GRID_W, GRID_W), NEG_BIG, np.float32)
    for qc in range(GRID_W):
        sc = min(max(qc - NA_WIN_W // 2, 0), GRID_W - NA_WIN_W)
        for kc in range(sc, sc + NA_WIN_W):
            shifts[kc - qc + NA_WIN_W - 1, qc, kc] = 1.0
            colmask[qc, kc] = 0.0
    toe = jnp.einsum("hrd,dqk->hrqk", rpb.astype(F32), shifts, precision=lax.Precision.HIGHEST) + colmask
    masked = jnp.full((heads, GRID_W, GRID_W), NEG_BIG, F32)
    cases = []
    for delta, r0 in zip(deltas, r0s):
        s0 = r0 + delta
        halves = []
        for qr in (r0, r0 + 1):
            sr = min(max(qr - NA_WIN_H // 2, 0), n_rows - NA_WIN_H)
            assert s0 <= sr and sr + NA_WIN_H <= s0 + NA_SLAB_ROWS
            halves.append(jnp.concatenate(
                [toe[:, kr - qr + NA_WIN_H - 1] if sr <= kr < sr + NA_WIN_H else masked
                 for kr in range(s0, s0 + NA_SLAB_ROWS)], axis=-1))
        cases.append(jnp.concatenate(halves, axis=1))
    cases.append(jnp.full((heads, NA_QB, NA_SLAB), NEG_BIG, F32))
    return jnp.stack(cases, axis=1)


NA_CTX_COL = -(-NA_SLAB // LANES) * LANES


def _natten_kernel(q_ref, k_ref, v_ref, tab_ref, o_ref, sa_ref, sb_ref, *, n_ctx, n_rows, scale):
    n_ctx_blocks = n_ctx // NA_QB
    n_blocks = q_ref.shape[0] // NA_QB

    def geometry(b):
        r0 = 2 * (b - n_ctx_blocks)
        s0 = jnp.minimum(jnp.clip(r0 - NA_WIN_H // 2, 0, n_rows - NA_WIN_H), n_rows - NA_SLAB_ROWS)
        delta = s0 - r0
        case = jnp.where(b < n_ctx_blocks, 5,
                         jnp.where(delta == 0, 0, jnp.where(delta == -2, 1, jnp.where(
                             delta == -4, 2, jnp.where(delta == -5, 3, 4)))))
        row0 = pl.multiple_of(n_ctx + jnp.where(b < n_ctx_blocks, 0, s0) * GRID_W, GRID_W)
        return case, row0, pl.multiple_of(b * NA_QB, NA_QB)

    def stage_scores(b, s_ref):
        _, row0, q0 = geometry(b)
        q = q_ref[pl.ds(q0, NA_QB), :]
        s_ref[:, 0:NA_SLAB] = lax.dot_general(q, k_ref[pl.ds(row0, NA_SLAB), :], NT_DIMS,
                                              preferred_element_type=F32)
        s_ref[:, NA_CTX_COL:NA_CTX_COL + n_ctx] = lax.dot_general(q, k_ref[0:n_ctx, :], NT_DIMS,
                                                                  preferred_element_type=F32)

    def attend(b, s_ref):
        case, row0, q0 = geometry(b)
        s_loc = s_ref[:, 0:NA_SLAB] * scale + tab_ref[0, case]
        s_ctx = s_ref[:, NA_CTX_COL:NA_CTX_COL + n_ctx] * scale
        m = jnp.maximum(jnp.max(s_loc, axis=-1, keepdims=True), jnp.max(s_ctx, axis=-1, keepdims=True))
        p_loc = jnp.exp(s_loc - m)
        p_ctx = jnp.exp(s_ctx - m)
        l = jnp.sum(p_loc, axis=-1, keepdims=True) + jnp.sum(p_ctx, axis=-1, keepdims=True)
        o = (jnp.dot(p_loc.astype(BF16), v_ref[pl.ds(row0, NA_SLAB), :], preferred_element_type=F32)
             + jnp.dot(p_ctx.astype(BF16), v_ref[0:n_ctx, :], preferred_element_type=F32))
        o_ref[pl.ds(q0, NA_QB), :] = (o / l).astype(o_ref.dtype)

    per_trip = _largest_divisor(n_blocks, (6, 2))
    bufs = (sa_ref, sb_ref)
    stage_scores(0, sa_ref)

    def body(j, carry):
        for i in range(per_trip):
            b = per_trip * j + i
            stage_scores(jnp.minimum(b + 1, n_blocks - 1), bufs[(i + 1) % 2])
            attend(b, bufs[i % 2])
        return carry

    lax.fori_loop(0, n_blocks // per_trip, body, 0)


def _natten(qkv, tables, *, n_ctx, n_rows):
    t = qkv.shape[0]
    heads, dh = NA_HEADS, LANES
    assert (t // NA_QB) % 2 == 0
    col = lambda off: pl.BlockSpec((t, dh), lambda h: (0, off + h))
    return pl.pallas_call(
        functools.partial(_natten_kernel, n_ctx=n_ctx, n_rows=n_rows, scale=dh ** -0.5),
        grid=(heads,),
        in_specs=[col(0), col(heads), col(2 * heads),
                  pl.BlockSpec((1, 6, NA_QB, NA_SLAB), lambda h: (h, 0, 0, 0))],
        out_specs=pl.BlockSpec((t, dh), lambda h: (0, h)),
        out_shape=jax.ShapeDtypeStruct((t, heads * dh), BF16),
        scratch_shapes=[pltpu.VMEM((NA_QB, NA_CTX_COL + n_ctx), F32)] * 2,
        compiler_params=_cparams(("parallel",), 2 * (4 * t * dh * 2 + 6 * NA_QB * NA_SLAB * 4)),
        name="natten",
    )(qkv, qkv, qkv, tables)


def _flash_kernel(*refs, groups, dq, dv, n_ctx, tk, tiles_per_trip, k_off, v_off, has_pe, shared_kv):
    refs = list(refs)
    q_ref = refs.pop(0)
    k_ref = refs.pop(0)
    kpe_ref = refs.pop(0) if has_pe else None
    v_ref = k_ref if shared_kv else refs.pop(0)
    o_ref, acc_ref, sa_ref, sb_ref = refs
    tq = q_ref.shape[0]
    t = k_ref.shape[0]
    qi = pl.program_id(1)
    n_main = jnp.where(qi * tq < n_ctx, 0, (t - n_ctx) // tk)

    def scores(g, start, size):
        kt = k_ref[pl.ds(start, size), k_off[g]:k_off[g] + LANES]
        if has_pe:
            kt = jnp.concatenate([kt, kpe_ref[pl.ds(start, size), :]], axis=1)
        return lax.dot_general(kt, q_ref[:, g * dq:(g + 1) * dq], NT_DIMS, preferred_element_type=F32)

    def pv(g, start, size, p):
        vt = v_ref[pl.ds(start, size), v_off[g]:v_off[g] + dv]
        return lax.dot_general(vt, p.astype(BF16), TN_DIMS, preferred_element_type=F32)

    carry = []
    ctx_scores = [scores(g, 0, n_ctx) for g in range(groups)]
    for g in range(groups):
        sa_ref[g] = scores(g, n_ctx, tk)
    for g in range(groups):
        s_t = ctx_scores[g]
        m0 = jnp.max(s_t, axis=0, keepdims=True)
        p = jnp.exp2(s_t - m0)
        carry += [m0, jnp.sum(p, axis=0, keepdims=True)]
        acc_ref[g] = pv(g, 0, n_ctx, p)

    def process(s_ref, start, carry):
        out = []
        for g in range(groups):
            m, l = carry[2 * g], carry[2 * g + 1]
            s_t = s_ref[g]
            m_new = jnp.maximum(m, jnp.max(s_t, axis=0, keepdims=True))
            alpha = jnp.exp2(m - m_new)
            p = jnp.exp2(s_t - m_new)
            out += [m_new, alpha * l + jnp.sum(p, axis=0, keepdims=True)]
            acc_ref[g] = alpha * acc_ref[g] + pv(g, start, tk, p)
        return tuple(out)

    def body(j, carry):
        base = n_ctx + tiles_per_trip * j * tk
        bufs = (sa_ref, sb_ref)
        for i in range(tiles_per_trip):
            cur = pl.multiple_of(base + i * tk, MXU_DIM)
            nxt = pl.multiple_of(jnp.minimum(base + (i + 1) * tk, t - tk), MXU_DIM)
            for g in range(groups):
                bufs[(i + 1) % 2][g] = scores(g, nxt, tk)
            carry = process(bufs[i % 2], cur, carry)
        return carry

    carry = lax.fori_loop(0, n_main // tiles_per_trip, body, tuple(carry))
    for g in range(groups):
        o_t = acc_ref[g] * (1.0 / carry[2 * g + 1])
        o_ref[:, g * dv:(g + 1) * dv] = o_t.T.astype(o_ref.dtype)


def _flash(q, k, v, kpe, *, n_steps, groups, dq, kv_width, k_off, v_off, n_ctx, tq, tk):
    t = q.shape[0]
    dv = LANES
    has_pe = kpe is not None
    shared_kv = v is None
    once = pl.Buffered(1)
    kv_spec = pl.BlockSpec((t, kv_width), lambda h, i: (0, h), pipeline_mode=once)
    in_specs = [pl.BlockSpec((tq, groups * dq), lambda h, i: (i, h)), kv_spec]
    args = [q, k]
    if has_pe:
        in_specs.append(pl.BlockSpec((t, LANES), lambda h, i: (0, 0), pipeline_mode=once))
        args.append(kpe)
    if not shared_kv:
        in_specs.append(kv_spec)
        args.append(v)
    tiles_per_trip = _largest_divisor((t - n_ctx) // tk, (8, 4, 2))
    vmem = (2 * tq * groups * (dq + dv) * 2 + (1 if shared_kv else 2) * t * kv_width * 2
            + t * LANES * 2 + groups * (dv * tq * 4 + 4 * tk * tq * 4))
    return pl.pallas_call(
        functools.partial(_flash_kernel, groups=groups, dq=dq, dv=dv, n_ctx=n_ctx, tk=tk,
                          tiles_per_trip=tiles_per_trip, k_off=k_off,
                          v_off=v_off, has_pe=has_pe, shared_kv=shared_kv),
        grid=(n_steps, t // tq),
        in_specs=in_specs,
        out_specs=pl.BlockSpec((tq, groups * dv), lambda h, i: (i, h)),
        out_shape=jax.ShapeDtypeStruct((t, n_steps * groups * dv), BF16),
        scratch_shapes=[pltpu.VMEM((groups, dv, tq), F32), pltpu.VMEM((groups, tk, tq), F32),
                        pltpu.VMEM((groups, tk, tq), F32)],
        compiler_params=_cparams(("parallel", "parallel"), vmem),
        name="flash",
    )(*args)


def _take_top(s, rowid, count):
    sentinel = s.shape[0]
    cur = s
    out = []
    for _ in range(count):
        m = jnp.max(cur, axis=0, keepdims=True)
        first = jnp.min(jnp.where(cur == m, rowid, sentinel), axis=0, keepdims=True)
        cur = jnp.where(rowid == first, -jnp.inf, cur)
        out.append(m)
    return out


_PEER_PAIRS = tuple((i, j) for i in range(PEER_TOPK) for j in range(PEER_TOPK) if (i + 1) * (j + 1) <= PEER_TOPK)


def _route_kernel(q_ref, sk_ref, a1_ref, b2_ref, tau_ref, cand_scr):
    tt = q_ref.shape[0]
    nk = sk_ref.shape[2]
    rowid = lax.broadcasted_iota(jnp.int32, (nk, tt), 0)
    n_pairs = len(_PEER_PAIRS)
    n_pad = -n_pairs % 8
    rowid_c = lax.broadcasted_iota(jnp.int32, (n_pairs + n_pad, tt), 0)

    def pair_sums(x1, x2, pad_value):
        for r, (i, j) in enumerate(_PEER_PAIRS):
            cand_scr[r:r + 1, :] = x1[i] + x2[j]
        for r in range(n_pairs, n_pairs + n_pad):
            cand_scr[r:r + 1, :] = jnp.full((1, tt), pad_value, F32)
        return cand_scr[...]

    def body(p, carry):
        def scores(hf):
            c0 = pl.multiple_of((2 * p + hf) * PEER_HALF, PEER_HALF)
            qs = q_ref[:, pl.ds(c0, PEER_HALF)].astype(BF16)
            keys = sk_ref[p, hf].astype(BF16)
            return lax.dot_general(keys, qs, NT_DIMS, preferred_element_type=F32)

        s1 = scores(0)
        s2 = scores(1)
        t1 = _take_top(s1, rowid, PEER_TOPK + 1)
        t2 = _take_top(s2, rowid, PEER_TOPK + 1)
        b1 = (s1 - t1[0]) * LOG2E
        b2 = (s2 - t2[0]) * LOG2E
        bt1 = [(x - t1[0]) * LOG2E for x in t1]
        bt2 = [(x - t2[0]) * LOG2E for x in t2]
        cand = pair_sums(bt1, bt2, -jnp.inf)
        selected = cand >= _take_top(cand, rowid_c, PEER_TOPK)[-1]
        log_z = jnp.log2(jnp.sum(jnp.where(selected, jnp.exp2(cand), 0.0), axis=0, keepdims=True))
        at1 = [x - log_z for x in bt1]
        shifted = pair_sums(at1, bt2, -jnp.inf)
        last_in = jnp.min(jnp.where(selected, shifted, jnp.inf), axis=0, keepdims=True)
        first_out = jnp.max(jnp.where(selected, -jnp.inf, shifted), axis=0, keepdims=True)
        first_out = jnp.maximum(first_out, jnp.maximum(at1[PEER_TOPK] + bt2[0], at1[0] + bt2[PEER_TOPK]))
        a1_ref[p] = b1 - log_z
        b2_ref[p] = b2
        tau_ref[p] = 0.5 * (last_in + first_out)
        return carry

    lax.fori_loop(0, PEER_HEADS, body, 0, unroll=2)


def _peer_route(q, sub_keys, *, tt):
    t = q.shape[0]
    heads, _, nk, half = sub_keys.shape
    big = pl.BlockSpec((heads, nk, tt), lambda i: (0, 0, i))
    big_shape = jax.ShapeDtypeStruct((heads, nk, t), F32)
    return pl.pallas_call(
        _route_kernel,
        grid=(t // tt,),
        in_specs=[pl.BlockSpec((tt, q.shape[1]), lambda i: (i, 0)),
                  pl.BlockSpec(sub_keys.shape, lambda i: (0, 0, 0, 0))],
        out_specs=[big, big, pl.BlockSpec((heads, 1, tt), lambda i: (0, 0, i))],
        out_shape=[big_shape, big_shape, jax.ShapeDtypeStruct((heads, 1, t), F32)],
        scratch_shapes=[pltpu.VMEM((len(_PEER_PAIRS) + (-len(_PEER_PAIRS) % 8), tt), F32)],
        compiler_params=_cparams(("parallel",), 2 * (tt * q.shape[1] * 4 + 2 * heads * nk * tt * 4)),
        name="peer_route",
    )(q, sub_keys)


PEER_K1_TILE = 8


def _peer_gate_kernel(h_ref, u_ref, a1_ref, b2_ref, tau_ref, o_ref, a_even, a_odd):
    s = pl.program_id(0)
    tt = h_ref.shape[0]
    heads, nk, _ = b2_ref.shape

    @pl.when(s == 0)
    def _():
        a_odd[...] = jnp.zeros_like(a_odd)

    def gated(a_read, a, ls):
        gates = jnp.zeros((nk, LANES), F32)
        for p in range(heads):
            c = a1_ref[p, a, :, ls] + b2_ref[p, :, ls]
            gates = gates + jnp.where(c >= tau_ref[p, :, ls], jnp.exp2(c), 0.0)
        act = a_read[a * nk:(a + 1) * nk, ls]
        gelu = 0.5 * act * (1.0 + lax.erf(act * (2.0 ** -0.5)))
        return gelu * gates

    def step(a_write, a_read):
        def chunk(n, carry):
            n0 = pl.multiple_of(n * MXU_DIM, MXU_DIM)
            a_write[:, pl.ds(n0, MXU_DIM)] = lax.dot_general(
                u_ref[...], h_ref[pl.ds(n0, MXU_DIM), :], NT_DIMS, preferred_element_type=F32)
            for c in range(MXU_DIM // LANES):
                ls = pl.ds(pl.multiple_of(n0 + c * LANES, LANES), LANES)
                cols = [gated(a_read, a, ls) for a in range(PEER_K1_TILE)]
                o_ref[ls, :] = jnp.concatenate(cols, axis=0).T.astype(o_ref.dtype)
            return carry

        lax.fori_loop(0, tt // MXU_DIM, chunk, 0)

    @pl.when(s % 2 == 0)
    def _():
        step(a_even, a_odd)

    @pl.when(s % 2 == 1)
    def _():
        step(a_odd, a_even)


def _peer_gate(h, u, a1, b2, tau, *, tt):
    t, d = h.shape
    heads, nk, _ = a1.shape
    te = PEER_K1_TILE * nk
    ne = u.shape[0] // te
    n_pairs = (t // tt) * ne
    cur = lambda s: jnp.minimum(s, n_pairs - 1)
    prev = lambda s: jnp.maximum(s - 1, 0)
    tok = pl.BlockSpec((heads, nk, tt), lambda s: (0, 0, prev(s) // ne))
    k1 = pl.BlockSpec((heads, PEER_K1_TILE, 1, tt), lambda s: (0, prev(s) % ne, 0, prev(s) // ne))
    a1 = a1.reshape(heads, nk, 1, t)
    vmem = (2 * tt * d * 2 + 2 * te * d * 2 + 2 * te * tt * 4 + 2 * tt * te * 2
            + 2 * heads * nk * tt * 4 + 2 * heads * PEER_K1_TILE * 8 * tt * 4)
    return pl.pallas_call(
        _peer_gate_kernel,
        grid=(n_pairs + 1,),
        in_specs=[pl.BlockSpec((tt, d), lambda s: (cur(s) // ne, 0)),
                  pl.BlockSpec((te, d), lambda s: (cur(s) % ne, 0)),
                  k1, tok,
                  pl.BlockSpec((heads, 1, tt), lambda s: (0, 0, prev(s) // ne))],
        out_specs=pl.BlockSpec((tt, te), lambda s: (prev(s) // ne, prev(s) % ne)),
        out_shape=jax.ShapeDtypeStruct((t, u.shape[0]), BF16),
        scratch_shapes=[pltpu.VMEM((te, tt), F32), pltpu.VMEM((te, tt), F32)],
        compiler_params=_cparams(("arbitrary",), vmem),
        name="peer_gate",
    )(h, u, a1, b2, tau)


def _rope_tables(n_ctx, seq, dim):
    quarter = dim // 4
    pos = jnp.arange(seq, dtype=jnp.int32)
    rows, cols = (pos // GRID_W).astype(F32), (pos % GRID_W).astype(F32)
    inv = ROPE_THETA ** (-jnp.arange(quarter, dtype=F32) / quarter)
    parts_c, parts_s = [], []
    for p in (rows, cols):
        ang = p[:, None] * inv[None, :]
        c, s = jnp.cos(ang), jnp.sin(ang)
        parts_c += [c, c]
        parts_s += [-s, s]
    cos = jnp.concatenate(parts_c, axis=1)
    sin = jnp.concatenate(parts_s, axis=1)
    reps = LANES // dim
    cos, sin = jnp.tile(cos, (1, reps)), jnp.tile(sin, (1, reps))
    cos = jnp.concatenate([jnp.ones((n_ctx, LANES), F32), cos], axis=0)
    sin = jnp.concatenate([jnp.zeros((n_ctx, LANES), F32), sin], axis=0)
    return cos, sin


def kernel(x, c, ctx, c_ctx, ada_w, ada_b, ln_g, ln_b, na_w_qkv, na_rpb, na_w_o, mla_w_dq, mla_q_norm, mla_w_uq, mla_w_dkv, mla_kv_norm, mla_w_ukv, mla_w_o, gqa_w_q, gqa_w_k, gqa_w_v, gqa_q_norm, gqa_k_norm, gqa_w_o, peer_w_query, peer_sub_keys, peer_u, peer_v):
    batch, seq, d = x.shape
    n_ctx = ctx.shape[1]
    depth = ada_w.shape[0]
    assert batch == 1 and seq % GRID_W == 0 and n_ctx % MXU_DIM == 0
    t = n_ctx + seq
    n_rows = seq // GRID_W
    alpha = (2 * depth) ** 0.25

    tm_row = 128
    tm_mm = _largest_divisor(t, (768, 512, 256))
    tq, tk = 256, 512
    tt_route = 256
    tt_dense = _largest_divisor(t, (768, 256))
    assert seq % tk == 0 and n_ctx % tq == 0 and n_rows >= 2 * NA_WIN_H
    ncb = n_ctx // tm_row

    xs = jnp.concatenate([ctx[0], x[0]], axis=0)
    cmat = jnp.zeros((8, d), F32).at[0].set(c_ctx).at[1].set(c[0])
    mod = _adaln(cmat, ada_w, ada_b)[:, :2].reshape(depth, 2, N_MOD, 1, d)
    mods = lambda i, k: mod[i, :, k]

    cos128, sin128 = _rope_tables(n_ctx, seq, GQA_HEAD_DIM)
    cos64, sin64 = _rope_tables(n_ctx, seq, MLA_ROPE)

    h = _modulate(xs, mods(0, 1), mods(0, 0), n_ctx_blocks=ncb, tm=tm_row)
    for i in range(depth):
        kind, j = i % N_MIXERS, i // N_MIXERS
        if kind == 0:
            qkv = _mm(h, na_w_qkv[j].astype(BF16), out_dtype=BF16, tm=tm_mm)
            o = _natten(qkv, _natten_tables(na_rpb[j], n_rows), n_ctx=n_ctx, n_rows=n_rows)
            y = _mm(o, na_w_o[j].astype(BF16), out_dtype=BF16, tm=tm_mm)
        elif kind == 1:
            qk_scale = (MLA_NOPE + MLA_ROPE) ** -0.5 * LOG2E
            rank = mla_w_uq.shape[1]
            w_uq = mla_w_uq[j].reshape(rank, MLA_HEADS, MLA_NOPE + MLA_ROPE)
            w_uq = jnp.pad(w_uq, ((0, 0), (0, 0), (0, MXU_DIM - MLA_NOPE - MLA_ROPE)))
            w_uq = w_uq.reshape(rank, MLA_HEADS * MXU_DIM).astype(BF16)
            w_pe = mla_w_dkv[j][:, MLA_KV_RANK:]
            w_dkv = jnp.concatenate([mla_w_dkv[j][:, :MLA_KV_RANK], w_pe, w_pe], axis=1).astype(BF16)
            cq = _mm(h, mla_w_dq[j].astype(BF16), out_dtype=F32, tm=tm_mm)
            q_raw = _mm(cq, w_uq, out_dtype=F32, tm=tm_mm, rms_gain=mla_q_norm[j])
            q = _rope_rows(q_raw, cos64, sin64, None, modes=("scale", "rope"), half=MLA_ROPE // 4,
                           scale=qk_scale, tm=tm_row)
            ckv = _mm(h, w_dkv, out_dtype=F32, tm=tm_mm)
            kv = _mm(ckv[:, :MLA_KV_RANK], mla_w_ukv[j].astype(BF16), out_dtype=BF16, tm=tm_mm,
                     rms_gain=mla_kv_norm[j])
            kpe = _rope_rows(ckv[:, MLA_KV_RANK:], cos64, sin64, None, modes=("rope",),
                             half=MLA_ROPE // 4, scale=1.0, tm=tm_row)
            hg = 4
            o = _flash(q, kv, None, kpe, n_steps=MLA_HEADS // hg, groups=hg, dq=MXU_DIM,
                       kv_width=hg * (MLA_NOPE + MLA_V),
                       k_off=tuple(g * (MLA_NOPE + MLA_V) for g in range(hg)),
                       v_off=tuple(g * (MLA_NOPE + MLA_V) + MLA_NOPE for g in range(hg)),
                       n_ctx=n_ctx, tq=tq, tk=tk)
            y = _mm(o, mla_w_o[j].astype(BF16), out_dtype=BF16, tm=tm_mm)
        else:
            qk_scale = GQA_HEAD_DIM ** -0.5 * LOG2E
            q_raw = _mm(h, gqa_w_q[j].astype(BF16), out_dtype=F32, tm=tm_mm)
            k_raw = _mm(h, gqa_w_k[j].astype(BF16), out_dtype=F32, tm=tm_mm)
            vv = _mm(h, gqa_w_v[j].astype(BF16), out_dtype=BF16, tm=tm_mm)
            q = _rope_rows(q_raw, cos128, sin128, gqa_q_norm[j], modes=("normrope",),
                           half=GQA_HEAD_DIM // 4, scale=qk_scale, tm=tm_row)
            kk = _rope_rows(k_raw, cos128, sin128, gqa_k_norm[j], modes=("normrope",),
                            half=GQA_HEAD_DIM // 4, scale=1.0, tm=tm_row)
            rep = GQA_HEADS // GQA_KV_HEADS
            o = _flash(q, kk, vv, None, n_steps=GQA_KV_HEADS, groups=rep, dq=GQA_HEAD_DIM,
                       kv_width=GQA_HEAD_DIM, k_off=(0,) * rep, v_off=(0,) * rep,
                       n_ctx=n_ctx, tq=tq, tk=tk)
            y = _mm(o, gqa_w_o[j].astype(BF16), out_dtype=BF16, tm=tm_mm)

        xs, h2 = _ln_mod(xs, y, mods(i, 2), ln_g[i, 0], ln_b[i, 0], mods(i, 4), mods(i, 3),
                         alpha=alpha, n_ctx_blocks=ncb, tm=tm_row)
        pq = _mm(h2, peer_w_query[i].astype(BF16), out_dtype=F32, tm=tm_mm)
        a1, b2, tau = _peer_route(pq, peer_sub_keys[i], tt=tt_route)
        w = _peer_gate(h2, peer_u[i].astype(BF16), a1, b2, tau, tt=tt_dense)
        f = _mm(w, peer_v[i].astype(BF16), out_dtype=BF16, tm=_largest_divisor(t, (384, 256)), tn=512)
        if i + 1 < depth:
            xs, h = _ln_mod(xs, f, mods(i, 5), ln_g[i, 1], ln_b[i, 1], mods(i + 1, 1), mods(i + 1, 0),
                            alpha=alpha, n_ctx_blocks=ncb, tm=tm_row)
        else:
            xs, _ = _ln_mod(xs, f, mods(i, 5), ln_g[i, 1], ln_b[i, 1], None, None,
                            alpha=alpha, n_ctx_blocks=ncb, tm=tm_row)
    return xs[n_ctx:][None]
```
